```python
import math
import jax, jax.numpy as jnp
from jax import lax
import numpy as np

D_MODEL = 2048
BATCH = 2
SEQ = 4096
DEPTH = 1

SB_HEADS = 8
SB_HEAD_DIM = 128
SB_WIDTH = SB_HEADS * SB_HEAD_DIM
MLA_HEADS = 8
MLA_NOPE_DIM = 128
MLA_ROPE_DIM = 64
MLA_V_DIM = 128
MLA_Q_RANK = 512
MLA_KV_RANK = 256
MLA_WIDTH = MLA_HEADS * MLA_V_DIM
MLA_QK_DIM = MLA_NOPE_DIM + MLA_ROPE_DIM
MIX_WIDTH = SB_WIDTH + MLA_WIDTH
ROPE_THETA = 10000.0
Q_BLOCK = 128
IN_PROJ_WIDTH = 3 * SB_WIDTH + MLA_Q_RANK + MLA_KV_RANK + MLA_ROPE_DIM
IN_SPLITS = [SB_WIDTH, 2 * SB_WIDTH, 3 * SB_WIDTH, 3 * SB_WIDTH + MLA_Q_RANK,
             3 * SB_WIDTH + MLA_Q_RANK + MLA_KV_RANK]
PEER_HEADS = 8
PEER_N_KEYS = 128
PEER_N_EXPERTS = PEER_N_KEYS * PEER_N_KEYS
PEER_QUERY_DIM = 256
PEER_HALF = PEER_QUERY_DIM // 2
PEER_TOPK = 16
PEER_CHUNK = 128
NORM_EPS = 1e-6

kernel_name = "hymba_sba_mla_peer_adaln_block"


def rmsnorm(x, g):
    xf = x.astype(jnp.float32)
    y = xf * lax.rsqrt(jnp.mean(xf * xf, axis=-1, keepdims=True) + NORM_EPS)
    return (y * g.astype(jnp.float32)).astype(x.dtype)


def rope_tables(positions, dtype):
    inv_freq = ROPE_THETA ** (-jnp.arange(0, MLA_ROPE_DIM, 2, dtype=jnp.float32) / MLA_ROPE_DIM)
    ang = positions.astype(jnp.float32)[..., None] * inv_freq
    return jnp.cos(ang).astype(dtype), jnp.sin(ang).astype(dtype)


def apply_rope(x, cos, sin):
    half = x.shape[-1] // 2
    x1, x2 = x[..., :half], x[..., half:]
    return jnp.concatenate([x1 * cos - x2 * sin, x2 * cos + x1 * sin], axis=-1)


def stick_breaking_attention(q, k, v):
    seq = q.shape[1]
    scale = SB_HEAD_DIM ** -0.5
    outs = []
    for blk in range(seq // Q_BLOCK):
        start = blk * Q_BLOCK
        end = start + Q_BLOCK
        z = jnp.einsum('bqhd,bkhd->bhqk', q[:, start:end], k[:, :end]).astype(jnp.float32) * scale
        t_idx = start + jnp.arange(Q_BLOCK)[:, None]
        s_idx = jnp.arange(end)[None, :]
        mask = s_idx < t_idx
        log_one_minus = jnp.where(mask, jax.nn.log_sigmoid(-z), 0.0)
        between = lax.cumsum(log_one_minus, axis=3, reverse=True) - log_one_minus
        weights = jnp.where(mask, jnp.exp(jax.nn.log_sigmoid(z) + between), 0.0)
        outs.append(jnp.einsum('bhqk,bkhd->bqhd', weights.astype(v.dtype), v[:, :end]))
    return jnp.concatenate(outs, axis=1)


def mla_attention(q_nope, q_pe, k_nope, k_pe, v):
    seq = q_nope.shape[1]
    scale = MLA_QK_DIM ** -0.5
    outs = []
    for blk in range(seq // Q_BLOCK):
        start = blk * Q_BLOCK
        end = start + Q_BLOCK
        s = (jnp.einsum('bqhd,bkhd->bhqk', q_nope[:, start:end], k_nope[:, :end])
             + jnp.einsum('bqhr,bkr->bhqk', q_pe[:, start:end], k_pe[:, :end]))
        s = s.astype(jnp.float32) * scale
        t_idx = start + jnp.arange(Q_BLOCK)[:, None]
        s_idx = jnp.arange(end)[None, :]
        s = jnp.where(s_idx <= t_idx, s, -jnp.inf)
        p = jax.nn.softmax(s, axis=-1)
        outs.append(jnp.einsum('bhqk,bkhd->bqhd', p.astype(v.dtype), v[:, :end]))
    return jnp.concatenate(outs, axis=1)


def peer_ffn(h, w_q, keys1, keys2, u_tab, v_tab):
    b, s, d = h.shape
    q = jnp.einsum('bsd,dq->bsq', h, w_q).reshape(b, s, PEER_HEADS, 2, PEER_HALF)
    s1 = jnp.einsum('bshq,hnq->bshn', q[..., 0, :], keys1)
    s2 = jnp.einsum('bshq,hnq->bshn', q[..., 1, :], keys2)
    v1, i1 = lax.top_k(s1, PEER_TOPK)
    v2, i2 = lax.top_k(s2, PEER_TOPK)
    cand = (v1[..., :, None] + v2[..., None, :]).reshape(b, s, PEER_HEADS, PEER_TOPK * PEER_TOPK)
    cand_idx = (i1[..., :, None] * PEER_N_KEYS + i2[..., None, :]).reshape(b, s, PEER_HEADS, PEER_TOPK * PEER_TOPK)
    best, pos = lax.top_k(cand, PEER_TOPK)
    expert_idx = jnp.take_along_axis(cand_idx, pos, axis=-1)
    gates = jax.nn.softmax(best.astype(jnp.float32), axis=-1).astype(h.dtype)
    n_blk = (b * s) // PEER_CHUNK
    hc = h.reshape(n_blk, PEER_CHUNK, d)
    ic = expert_idx.reshape(n_blk, PEER_CHUNK, PEER_HEADS, PEER_TOPK)
    gc = gates.reshape(n_blk, PEER_CHUNK, PEER_HEADS, PEER_TOPK)

    def block_fn(args):
        hx, ix, gx = args
        u = jnp.take(u_tab, ix, axis=0)
        act = jax.nn.gelu(jnp.einsum('chkd,cd->chk', u, hx), approximate=False)
        vv = jnp.take(v_tab, ix, axis=0)
        return jnp.einsum('chk,chkd->cd', gx * act, vv)

    y = lax.map(block_fn, (hc, ic, gc))
    return y.reshape(b, s, d)


def setup_inputs(seed: int = 0) -> dict:
    key = jax.random.key(seed)
    ks = jax.random.split(key, 24)
    f32 = jnp.float32
    D = D_MODEL

    def nrm(k, shape, scale):
        return jax.random.normal(k, shape, f32) * scale

    def gain(k, shape):
        return 1.0 + 0.02 * jax.random.normal(k, shape, f32)

    offset = jax.random.randint(ks[2], (BATCH, 1), 0, 1024, dtype=jnp.int32)
    positions = (offset + jnp.arange(SEQ, dtype=jnp.int32)[None, :]).astype(jnp.int32)
    return {
        "x": nrm(ks[0], (BATCH, SEQ, D), 1.0),
        "c": nrm(ks[1], (BATCH, D), 1.0),
        "positions": positions,
        "w_ada": nrm(ks[3], (DEPTH, D, 6 * D), D ** -0.5),
        "b_ada": nrm(ks[4], (DEPTH, 6 * D), 0.02),
        "norm1_g": gain(ks[5], (DEPTH, D)),
        "w_in": nrm(ks[6], (DEPTH, D, IN_PROJ_WIDTH), D ** -0.5),
        "mla_q_norm_g": gain(ks[7], (DEPTH, MLA_Q_RANK)),
        "w_uq": nrm(ks[8], (DEPTH, MLA_Q_RANK, MLA_HEADS * MLA_QK_DIM), MLA_Q_RANK ** -0.5),
        "mla_kv_norm_g": gain(ks[9], (DEPTH, MLA_KV_RANK)),
        "w_ukv": nrm(ks[10], (DEPTH, MLA_KV_RANK, MLA_HEADS * (MLA_NOPE_DIM + MLA_V_DIM)), MLA_KV_RANK ** -0.5),
        "out_norm_sba_g": gain(ks[11], (DEPTH, SB_WIDTH)),
        "out_norm_mla_g": gain(ks[12], (DEPTH, MLA_WIDTH)),
        "w_o": nrm(ks[13], (DEPTH, MIX_WIDTH, D), MIX_WIDTH ** -0.5),
        "norm2_g": gain(ks[14], (DEPTH, D)),
        "w_peer_q": nrm(ks[15], (DEPTH, D, PEER_HEADS * PEER_QUERY_DIM), D ** -0.5),
        "peer_keys1": nrm(ks[16], (DEPTH, PEER_HEADS, PEER_N_KEYS, PEER_HALF), PEER_HALF ** -0.5),
        "peer_keys2": nrm(ks[17], (DEPTH, PEER_HEADS, PEER_N_KEYS, PEER_HALF), PEER_HALF ** -0.5),
        "peer_u": nrm(ks[18], (DEPTH, PEER_N_EXPERTS, D), D ** -0.5),
        "peer_v": nrm(ks[19], (DEPTH, PEER_N_EXPERTS, D), PEER_HEADS ** -0.5),
        "final_norm_g": gain(ks[20], (D,)),
    }


def reference(x, c, positions, w_ada, b_ada, norm1_g, w_in, mla_q_norm_g, w_uq, mla_kv_norm_g, w_ukv,
              out_norm_sba_g, out_norm_mla_g, w_o, norm2_g, w_peer_q, peer_keys1, peer_keys2,
              peer_u, peer_v, final_norm_g):
    b, s, d = x.shape
    cos, sin = rope_tables(positions, x.dtype)
    c_act = jax.nn.silu(c)
    for l in range(DEPTH):
        mod = (c_act @ w_ada[l] + b_ada[l])[:, None, :]
        sh1, sc1, g1, sh2, sc2, g2 = jnp.split(mod, 6, axis=-1)

        h = rmsnorm(x, norm1_g[l]) * (1.0 + sc1) + sh1
        proj = h @ w_in[l]
        sq, sk, sv, cq, ckv, kr = jnp.split(proj, IN_SPLITS, axis=-1)

        sb_out = stick_breaking_attention(
            sq.reshape(b, s, SB_HEADS, SB_HEAD_DIM),
            sk.reshape(b, s, SB_HEADS, SB_HEAD_DIM),
            sv.reshape(b, s, SB_HEADS, SB_HEAD_DIM)).reshape(b, s, SB_WIDTH)

        q = (rmsnorm(cq, mla_q_norm_g[l]) @ w_uq[l]).reshape(b, s, MLA_HEADS, MLA_QK_DIM)
        q_nope = q[..., :MLA_NOPE_DIM]
        q_pe = apply_rope(q[..., MLA_NOPE_DIM:], cos[:, :, None, :], sin[:, :, None, :])
        kv = (rmsnorm(ckv, mla_kv_norm_g[l]) @ w_ukv[l]).reshape(b, s, MLA_HEADS, MLA_NOPE_DIM + MLA_V_DIM)
        k_nope = kv[..., :MLA_NOPE_DIM]
        v_mla = kv[..., MLA_NOPE_DIM:]
        k_pe = apply_rope(kr, cos, sin)
        mla_out = mla_attention(q_nope, q_pe, k_nope, k_pe, v_mla).reshape(b, s, MLA_WIDTH)

        mixed = jnp.concatenate([rmsnorm(sb_out, out_norm_sba_g[l]),
                                 rmsnorm(mla_out, out_norm_mla_g[l])], axis=-1) @ w_o[l]
        x = x + g1 * mixed

        h2 = rmsnorm(x, norm2_g[l]) * (1.0 + sc2) + sh2
        x = x + g2 * peer_ffn(h2, w_peer_q[l], peer_keys1[l], peer_keys2[l], peer_u[l], peer_v[l])
    return rmsnorm(x, final_norm_g)
```

```python
import functools
import math

import jax
import jax.numpy as jnp
from jax import lax
from jax.experimental import pallas as pl
from jax.experimental.pallas import tpu as pltpu

F32 = jnp.float32
BF16 = jnp.bfloat16

SB_HEADS = 8
SB_HEAD_DIM = 128
SB_WIDTH = SB_HEADS * SB_HEAD_DIM
MLA_HEADS = 8
MLA_NOPE_DIM = 128
MLA_ROPE_DIM = 64
MLA_V_DIM = 128
MLA_Q_RANK = 512
MLA_KV_RANK = 256
MLA_QK_DIM = MLA_NOPE_DIM + MLA_ROPE_DIM
MLA_WIDTH = MLA_HEADS * MLA_V_DIM
ROPE_THETA = 10000.0
PEER_HEADS = 8
PEER_N_KEYS = 128
PEER_HALF = 128
PEER_TOPK = 16
NORM_EPS = 1e-6
SB_SCALE = SB_HEAD_DIM ** -0.5
MLA_SCALE = MLA_QK_DIM ** -0.5

LANES = 128
V7X_VMEM_BYTES = 64 * 1024 * 1024
MLA_HEAD_PAD = 2 * LANES
LATENT_WIDTH = MLA_Q_RANK + MLA_KV_RANK + 2 * LANES
STAT_ROWS = 8


def _tile(n, pref):
    if n <= pref:
        return n
    t = pref
    while n % t:
        t -= 1
    return t


def _params(semantics, vmem_mib):
    return pltpu.CompilerParams(dimension_semantics=semantics,
                                vmem_limit_bytes=min(vmem_mib * 1024 * 1024, V7X_VMEM_BYTES - 4 * 1024 * 1024))


def _rms(x):
    return x * lax.rsqrt(jnp.mean(x * x, axis=-1, keepdims=True) + NORM_EPS)


def _dot(a, b):
    return jnp.dot(a, b, preferred_element_type=F32)


def _dot_nt(a, b):
    return lax.dot_general(a, b, (((1,), (1,)), ((), ())), preferred_element_type=F32)


def _adaln_kernel(c_ref, w_ref, b_ref, o_ref):
    c = c_ref[...]
    c_act = c / (1.0 + jnp.exp(-c))
    o_ref[...] = _dot(c_act.astype(BF16), w_ref[...].astype(BF16)) + b_ref[...]


def _adaln(c_pad, w, b):
    rows, d = c_pad.shape
    n = w.shape[1]
    tn = _tile(n, 1024)
    return pl.pallas_call(
        _adaln_kernel,
        out_shape=jax.ShapeDtypeStruct((rows, n), F32),
        grid=(n // tn,),
        in_specs=[pl.BlockSpec((rows, d), lambda j: (0, 0)),
                  pl.BlockSpec((d, tn), lambda j: (0, j)),
                  pl.BlockSpec((1, tn), lambda j: (0, j))],
        out_specs=pl.BlockSpec((rows, tn), lambda j: (0, j)),
        compiler_params=_params(("arbitrary",), 40),
        name="adaln",
    )(c_pad, w, b)


def _inproj_kernel(x_ref, mod_ref, g_ref, w_ref, oa_ref, ob_ref, h_scr, *, n_head_tiles):
    j = pl.program_id(1)

    @pl.when(j == 0)
    def _():
        y = _rms(x_ref[...]) * g_ref[...]
        h_scr[...] = (y * (1.0 + mod_ref[0, 1:2, :]) + mod_ref[0, 0:1, :]).astype(BF16)

    acc = _dot(h_scr[...], w_ref[...])

    @pl.when(j == 0)
    def _():
        oa_ref[...] = (acc * SB_SCALE).astype(BF16)

    @pl.when(jnp.logical_and(j > 0, j < n_head_tiles))
    def _():
        oa_ref[...] = acc.astype(BF16)

    @pl.when(j == n_head_tiles)
    def _():
        ob_ref[...] = acc


def _inproj(x2d, mod3, g, w_ext, seq):
    t, d = x2d.shape
    tn = SB_WIDTH
    n_head_tiles = 3
    assert w_ext.shape[1] == n_head_tiles * tn + LATENT_WIDTH and LATENT_WIDTH == tn
    tm = _tile(seq, 512)
    return pl.pallas_call(
        functools.partial(_inproj_kernel, n_head_tiles=n_head_tiles),
        out_shape=(jax.ShapeDtypeStruct((t, n_head_tiles * tn), BF16),
                   jax.ShapeDtypeStruct((t, LATENT_WIDTH), F32)),
        grid=(t // tm, n_head_tiles + 1),
        in_specs=[pl.BlockSpec((tm, d), lambda i, j: (i, 0)),
                  pl.BlockSpec((1, 6, d), lambda i, j: ((i * tm) // seq, 0, 0)),
                  pl.BlockSpec((1, d), lambda i, j: (0, 0)),
                  pl.BlockSpec((d, tn), lambda i, j: (0, j))],
        out_specs=(pl.BlockSpec((tm, tn), lambda i, j: (i, jnp.minimum(j, n_head_tiles - 1))),
                   pl.BlockSpec((tm, LATENT_WIDTH), lambda i, j: (i, 0))),
        scratch_shapes=[pltpu.VMEM((tm, d), BF16)],
        compiler_params=_params(("parallel", "arbitrary"), 48),
        name="inproj",
    )(x2d, mod3, g, w_ext)


def _mla_proj_kernel(c_ref, pos_ref, invf_ref, gq_ref, gkv_ref, wq_ref, wkn_ref, wv_ref,
                     q_ref, k_ref, v_ref):
    cqn = (_rms(c_ref[:, :MLA_Q_RANK]) * gq_ref[...]).astype(BF16)
    ckvn = (_rms(c_ref[:, MLA_Q_RANK:MLA_Q_RANK + MLA_KV_RANK]) * gkv_ref[...]).astype(BF16)
    kr = c_ref[:, MLA_Q_RANK + MLA_KV_RANK:MLA_Q_RANK + MLA_KV_RANK + LANES]
    kr_rot = c_ref[:, MLA_Q_RANK + MLA_KV_RANK + LANES:]
    ang = pos_ref[...] * invf_ref[...]
    cos = jnp.cos(ang)
    sin = jnp.sin(ang)
    k_pe = (kr * cos + kr_rot * sin).astype(BF16)
    qa = _dot(cqn, wq_ref[...])
    kn = _dot(ckvn, wkn_ref[...])
    v_ref[...] = _dot(ckvn, wv_ref[...]).astype(BF16)
    for h in range(MLA_HEADS):
        b = 3 * LANES * h
        o = MLA_HEAD_PAD * h
        q_ref[:, o:o + LANES] = (qa[:, b:b + LANES] * MLA_SCALE).astype(BF16)
        q_pe = qa[:, b + LANES:b + 2 * LANES] * cos + qa[:, b + 2 * LANES:b + 3 * LANES] * sin
        q_ref[:, o + LANES:o + 2 * LANES] = (q_pe * MLA_SCALE).astype(BF16)
        k_ref[:, o:o + LANES] = kn[:, h * LANES:(h + 1) * LANES].astype(BF16)
        k_ref[:, o + LANES:o + 2 * LANES] = k_pe


def _mla_proj(latent, pos, invf, gq, gkv, wq, wkn, wv):
    t = latent.shape[0]
    tm = _tile(t, 512)
    full = lambda a: pl.BlockSpec(a.shape, lambda i: (0,) * a.ndim)
    return pl.pallas_call(
        _mla_proj_kernel,
        out_shape=(jax.ShapeDtypeStruct((t, MLA_HEADS * MLA_HEAD_PAD), BF16),
                   jax.ShapeDtypeStruct((t, MLA_HEADS * MLA_HEAD_PAD), BF16),
                   jax.ShapeDtypeStruct((t, MLA_WIDTH), BF16)),
        grid=(t // tm,),
        in_specs=[pl.BlockSpec((tm, LATENT_WIDTH), lambda i: (i, 0)),
                  pl.BlockSpec((tm, 1), lambda i: (i, 0)),
                  full(invf), full(gq), full(gkv), full(wq), full(wkn), full(wv)],
        out_specs=(pl.BlockSpec((tm, MLA_HEADS * MLA_HEAD_PAD), lambda i: (i, 0)),
                   pl.BlockSpec((tm, MLA_HEADS * MLA_HEAD_PAD), lambda i: (i, 0)),
                   pl.BlockSpec((tm, MLA_WIDTH), lambda i: (i, 0))),
        compiler_params=_params(("parallel",), 48),
        name="mla_proj",
    )(latent, pos, invf, gq, gkv, wq, wkn, wv)


def _sb_attn_kernel(q_ref, k_ref, v_ref, o_ref, *, tb):
    qi = pl.program_id(1)
    q = q_ref[...]
    row = lax.broadcasted_iota(jnp.int32, (tb, tb), 0)
    col = lax.broadcasted_iota(jnp.int32, (tb, tb), 1)
    later = (row > col).astype(BF16)
    causal = col < row

    def block(kb, carry, masked):
        acc, run = carry
        start = pl.multiple_of(kb * tb, tb)
        k = k_ref[pl.ds(start, tb), :]
        v = v_ref[pl.ds(start, tb), :]
        z = _dot_nt(q, k)
        t = jnp.log1p(jnp.exp(-jnp.abs(z)))
        log_1m = -(jnp.maximum(z, 0.0) + t)
        log_b = jnp.minimum(z, 0.0) - t
        if masked:
            log_1m = jnp.where(causal, log_1m, 0.0)
        hi = log_1m.astype(BF16)
        r1 = log_1m - hi.astype(F32)
        mid = r1.astype(BF16)
        lo = (r1 - mid.astype(F32)).astype(BF16)
        between = _dot(hi, later) + _dot(mid, later) + _dot(lo, later)
        w = jnp.exp(log_b + between + run)
        if masked:
            w = jnp.where(causal, w, 0.0)
        acc = acc + _dot(w.astype(BF16), v)
        run = run + jnp.sum(log_1m, axis=1, keepdims=True)
        return acc, run

    carry = (jnp.zeros((tb, SB_HEAD_DIM), F32), jnp.zeros((tb, 1), F32))
    carry = block(qi, carry, True)
    carry = lax.fori_loop(0, qi, lambda i, c: block(qi - 1 - i, c, False), carry)
    o_ref[...] = carry[0]


def _sb_attn(qkv, batch, seq):
    t = qkv.shape[0]
    tb = _tile(seq, 256)
    nq = seq // tb
    return pl.pallas_call(
        functools.partial(_sb_attn_kernel, tb=tb),
        out_shape=jax.ShapeDtypeStruct((t, SB_WIDTH), F32),
        grid=(batch * SB_HEADS, nq),
        in_specs=[pl.BlockSpec((tb, SB_HEAD_DIM), lambda g, i: ((g // SB_HEADS) * nq + i, g % SB_HEADS)),
                  pl.BlockSpec((seq, SB_HEAD_DIM), lambda g, i: (g // SB_HEADS, SB_HEADS + g % SB_HEADS)),
                  pl.BlockSpec((seq, SB_HEAD_DIM), lambda g, i: (g // SB_HEADS, 2 * SB_HEADS + g % SB_HEADS))],
        out_specs=pl.BlockSpec((tb, SB_HEAD_DIM), lambda g, i: ((g // SB_HEADS) * nq + i, g % SB_HEADS)),
        compiler_params=_params(("parallel", "arbitrary"), 32),
        name="sb_attn",
    )(qkv, qkv, qkv)


def _mla_attn_kernel(q_ref, k_ref, v_ref, o_ref, *, tb):
    qi = pl.program_id(1)
    q = q_ref[...]
    row = lax.broadcasted_iota(jnp.int32, (tb, tb), 0)
    col = lax.broadcasted_iota(jnp.int32, (tb, tb), 1)
    causal = col <= row

    def block(kb, carry, masked):
        acc, m, l = carry
        start = pl.multiple_of(kb * tb, tb)
        s = _dot_nt(q, k_ref[pl.ds(start, tb), :])
        if masked:
            s = jnp.where(causal, s, -jnp.inf)
        m_new = jnp.maximum(m, jnp.max(s, axis=1, keepdims=True))
        alpha = jnp.exp(m - m_new)
        p = jnp.exp(s - m_new)
        l = alpha * l + jnp.sum(p, axis=1, keepdims=True)
        acc = alpha * acc + _dot(p.astype(BF16), v_ref[pl.ds(start, tb), :])
        return acc, m_new, l

    carry = (jnp.zeros((tb, MLA_V_DIM), F32), jnp.full((tb, 1), -jnp.inf, F32), jnp.zeros((tb, 1), F32))
    carry = block(qi, carry, True)
    acc, _, l = lax.fori_loop(0, qi, lambda i, c: block(qi - 1 - i, c, False), carry)
    o_ref[...] = acc / l


def _mla_attn(q, k, v, batch, seq):
    t = q.shape[0]
    tb = _tile(seq, 256)
    nq = seq // tb
    return pl.pallas_call(
        functools.partial(_mla_attn_kernel, tb=tb),
        out_shape=jax.ShapeDtypeStruct((t, MLA_WIDTH), F32),
        grid=(batch * MLA_HEADS, nq),
        in_specs=[pl.BlockSpec((tb, MLA_HEAD_PAD), lambda g, i: ((g // MLA_HEADS) * nq + i, g % MLA_HEADS)),
                  pl.BlockSpec((seq, MLA_HEAD_PAD), lambda g, i: (g // MLA_HEADS, g % MLA_HEADS)),
                  pl.BlockSpec((seq, MLA_V_DIM), lambda g, i: (g // MLA_HEADS, g % MLA_HEADS))],
        out_specs=pl.BlockSpec((tb, MLA_V_DIM), lambda g, i: ((g // MLA_HEADS) * nq + i, g % MLA_HEADS)),
        compiler_params=_params(("parallel", "arbitrary"), 32),
        name="mla_attn",
    )(q, k, v)


def _mix_kernel(sb_ref, ml_ref, x_ref, mod_ref, gsb_ref, gml_ref, g2_ref, wo_ref, x1_ref, h2_ref):
    a = (_rms(sb_ref[...]) * gsb_ref[...]).astype(BF16)
    b = (_rms(ml_ref[...]) * gml_ref[...]).astype(BF16)
    mixed = _dot(a, wo_ref[:SB_WIDTH, :]) + _dot(b, wo_ref[SB_WIDTH:, :])
    x1 = x_ref[...] + mod_ref[0, 2:3, :] * mixed
    x1_ref[...] = x1
    h2 = _rms(x1) * g2_ref[...]
    h2_ref[...] = (h2 * (1.0 + mod_ref[0, 4:5, :]) + mod_ref[0, 3:4, :]).astype(BF16)


def _mix(sb, ml, x2d, mod3, gsb, gml, g2, wo, seq):
    t, d = x2d.shape
    tm = _tile(seq, 256)
    full = lambda a: pl.BlockSpec(a.shape, lambda i: (0,) * a.ndim)
    return pl.pallas_call(
        _mix_kernel,
        out_shape=(jax.ShapeDtypeStruct((t, d), F32), jax.ShapeDtypeStruct((t, d), BF16)),
        grid=(t // tm,),
        in_specs=[pl.BlockSpec((tm, SB_WIDTH), lambda i: (i, 0)),
                  pl.BlockSpec((tm, MLA_WIDTH), lambda i: (i, 0)),
                  pl.BlockSpec((tm, d), lambda i: (i, 0)),
                  pl.BlockSpec((1, 6, d), lambda i: ((i * tm) // seq, 0, 0)),
                  full(gsb), full(gml), full(g2), full(wo)],
        out_specs=(pl.BlockSpec((tm, d), lambda i: (i, 0)), pl.BlockSpec((tm, d), lambda i: (i, 0))),
        compiler_params=_params(("parallel",), 48),
        name="mix",
    )(sb, ml, x2d, mod3, gsb, gml, g2, wo)


def _top_values(s, k):
    vals = []
    for _ in range(k):
        mx = jnp.max(s, axis=0, keepdims=True)
        vals.append(mx)
        s = jnp.where(s == mx, -jnp.inf, s)
    return vals


def _peer_route_kernel(h2_ref, wq_ref, k1_ref, k2_ref, s1_ref, s2_ref, st_ref):
    qp = _dot(h2_ref[...], wq_ref[...]).astype(BF16)
    tm = qp.shape[0]
    for h in range(PEER_HEADS):
        o = 2 * PEER_HALF * h
        s1 = _dot_nt(k1_ref[h], qp[:, o:o + PEER_HALF])
        s2 = _dot_nt(k2_ref[h], qp[:, o + PEER_HALF:o + 2 * PEER_HALF])
        s1_ref[h] = s1
        s2_ref[h] = s2
        v1 = _top_values(s1, PEER_TOPK)
        v2 = _top_values(s2, PEER_TOPK)
        cand = jnp.concatenate([v1[a] + v2[b] for a in range(PEER_TOPK) for b in range(PEER_TOPK)
                                if (a + 1) * (b + 1) <= PEER_TOPK], axis=0)
        best = _top_values(cand, PEER_TOPK)
        z = best[0] * 0.0
        for c in best:
            z = z + jnp.exp(c - best[0])
        st_ref[h] = jnp.concatenate([best[-1], v1[0], v2[0], 1.0 / z,
                                     jnp.zeros((STAT_ROWS - 4, tm), F32)], axis=0)


def _peer_route(h2, wq, k1, k2):
    t, d = h2.shape
    tm = _tile(t, 256)
    full = lambda a: pl.BlockSpec(a.shape, lambda i: (0,) * a.ndim)
    return pl.pallas_call(
        _peer_route_kernel,
        out_shape=(jax.ShapeDtypeStruct((PEER_HEADS, PEER_N_KEYS, t), F32),
                   jax.ShapeDtypeStruct((PEER_HEADS, PEER_N_KEYS, t), F32),
                   jax.ShapeDtypeStruct((PEER_HEADS, STAT_ROWS, t), F32)),
        grid=(t // tm,),
        in_specs=[pl.BlockSpec((tm, d), lambda i: (i, 0)), full(wq), full(k1), full(k2)],
        out_specs=(pl.BlockSpec((PEER_HEADS, PEER_N_KEYS, tm), lambda i: (0, 0, i)),
                   pl.BlockSpec((PEER_HEADS, PEER_N_KEYS, tm), lambda i: (0, 0, i)),
                   pl.BlockSpec((PEER_HEADS, STAT_ROWS, tm), lambda i: (0, 0, i))),
        compiler_params=_params(("parallel",), 48),
        name="peer_route",
    )(h2, wq, k1, k2)


def _peer_dense_kernel(h2_ref, u_ref, vt_ref, s1_ref, s2_ref, st_ref, o_ref, acc_ref, e1_ref, e2_ref, *, rows):
    j = pl.program_id(1)

    @pl.when(j == 0)
    def _():
        acc_ref[...] = jnp.zeros_like(acc_ref)
        for h in range(PEER_HEADS):
            e1_ref[h] = jnp.exp(s1_ref[h] - st_ref[h, 1:2, :])
            e2_ref[h] = jnp.exp(s2_ref[h] - st_ref[h, 2:3, :]) * st_ref[h, 3:4, :]

    st = _dot_nt(u_ref[...], h2_ref[...])
    act = 0.5 * st * (1.0 + lax.erf(st * (2.0 ** -0.5)))
    gates = []
    for r in range(rows):
        i1 = j * rows + r
        g = None
        for h in range(PEER_HEADS):
            pair = s2_ref[h] + s1_ref[h, pl.ds(i1, 1), :]
            term = jnp.where(pair >= st_ref[h, 0:1, :], e2_ref[h], 0.0) * e1_ref[h, pl.ds(i1, 1), :]
            g = term if g is None else g + term
        gates.append(g)
    gate = jnp.concatenate(gates, axis=0)
    acc_ref[...] += _dot(vt_ref[...], (act * gate).astype(BF16))

    @pl.when(j == pl.num_programs(1) - 1)
    def _():
        o_ref[...] = acc_ref[...].T


def _peer_dense(h2, u, vt, s1, s2, st):
    t, d = h2.shape
    n_exp = u.shape[0]
    tm = _tile(t, 512)
    te = 512
    return pl.pallas_call(
        functools.partial(_peer_dense_kernel, rows=te // PEER_N_KEYS),
        out_shape=jax.ShapeDtypeStruct((t, d), F32),
        grid=(t // tm, n_exp // te),
        in_specs=[pl.BlockSpec((tm, d), lambda i, j: (i, 0)),
                  pl.BlockSpec((te, d), lambda i, j: (j, 0)),
                  pl.BlockSpec((d, te), lambda i, j: (0, j)),
                  pl.BlockSpec((PEER_HEADS, PEER_N_KEYS, tm), lambda i, j: (0, 0, i)),
                  pl.BlockSpec((PEER_HEADS, PEER_N_KEYS, tm), lambda i, j: (0, 0, i)),
                  pl.BlockSpec((PEER_HEADS, STAT_ROWS, tm), lambda i, j: (0, 0, i))],
        out_specs=pl.BlockSpec((tm, d), lambda i, j: (i, 0)),
        scratch_shapes=[pltpu.VMEM((d, tm), F32),
                        pltpu.VMEM((PEER_HEADS, PEER_N_KEYS, tm), F32),
                        pltpu.VMEM((PEER_HEADS, PEER_N_KEYS, tm), F32)],
        compiler_params=_params(("parallel", "arbitrary"), 56),
        name="peer_dense",
    )(h2, u, vt, s1, s2, st)


def _residual_kernel(x1_ref, p_ref, mod_ref, gf_ref, o_ref, *, final_norm):
    x2 = x1_ref[...] + mod_ref[0, 5:6, :] * p_ref[...]
    o_ref[...] = _rms(x2) * gf_ref[...] if final_norm else x2


def _residual(x1, peer, mod3, gf, seq, final_norm):
    t, d = x1.shape
    tm = _tile(seq, 512)
    return pl.pallas_call(
        functools.partial(_residual_kernel, final_norm=final_norm),
        out_shape=jax.ShapeDtypeStruct((t, d), F32),
        grid=(t // tm,),
        in_specs=[pl.BlockSpec((tm, d), lambda i: (i, 0)),
                  pl.BlockSpec((tm, d), lambda i: (i, 0)),
                  pl.BlockSpec((1, 6, d), lambda i: ((i * tm) // seq, 0, 0)),
                  pl.BlockSpec((1, d), lambda i: (0, 0))],
        out_specs=pl.BlockSpec((tm, d), lambda i: (i, 0)),
        compiler_params=_params(("parallel",), 40),
        name="residual",
    )(x1, peer, mod3, gf)


def _rot_half_cols(w):
    half = w.shape[-1] // 2
    return jnp.concatenate([-w[..., half:], w[..., :half]], axis=-1)


def _pad_cols(w, width):
    return jnp.concatenate([w, jnp.zeros(w.shape[:-1] + (width - w.shape[-1],), w.dtype)], axis=-1)


def _layer_weights(w_in, w_uq, w_ukv):
    d = w_in.shape[0]
    n_heads_cols = 3 * SB_WIDTH
    w_kr = w_in[:, n_heads_cols + MLA_Q_RANK + MLA_KV_RANK:]
    w_ext = jnp.concatenate([w_in[:, :n_heads_cols + MLA_Q_RANK + MLA_KV_RANK],
                             _pad_cols(w_kr, LANES), _pad_cols(_rot_half_cols(w_kr), LANES)], axis=1)
    wq = w_uq.reshape(MLA_Q_RANK, MLA_HEADS, MLA_QK_DIM)
    wq_pe = wq[..., MLA_NOPE_DIM:]
    wq_ext = jnp.concatenate([wq[..., :MLA_NOPE_DIM], _pad_cols(wq_pe, LANES),
                              _pad_cols(_rot_half_cols(wq_pe), LANES)], axis=-1)
    wq_ext = wq_ext.reshape(MLA_Q_RANK, MLA_HEADS * 3 * LANES)
    wkv = w_ukv.reshape(MLA_KV_RANK, MLA_HEADS, MLA_NOPE_DIM + MLA_V_DIM)
    wkn = wkv[..., :MLA_NOPE_DIM].reshape(MLA_KV_RANK, MLA_HEADS * MLA_NOPE_DIM)
    wv = wkv[..., MLA_NOPE_DIM:].reshape(MLA_KV_RANK, MLA_WIDTH)
    del d
    return w_ext.astype(BF16), wq_ext.astype(BF16), wkn.astype(BF16), wv.astype(BF16)


def kernel(x, c, positions, w_ada, b_ada, norm1_g, w_in, mla_q_norm_g, w_uq, mla_kv_norm_g, w_ukv,
           out_norm_sba_g, out_norm_mla_g, w_o, norm2_g, w_peer_q, peer_keys1, peer_keys2,
           peer_u, peer_v, final_norm_g):
    batch, seq, d = x.shape
    depth = w_ada.shape[0]
    t = batch * seq
    x2d = x.reshape(t, d)
    pos = positions.reshape(t, 1).astype(F32)
    inv_freq = ROPE_THETA ** (-jnp.arange(0, MLA_ROPE_DIM, 2, dtype=F32) / MLA_ROPE_DIM)
    invf = _pad_cols(jnp.concatenate([inv_freq, inv_freq])[None, :], LANES)
    c_rows = 16
    c_pad = jnp.concatenate([c, jnp.zeros((c_rows - batch, d), c.dtype)], axis=0)
    row = lambda g: g[None, :]

    for l in range(depth):
        mod = _adaln(c_pad, w_ada[l], b_ada[l][None, :])
        mod3 = mod[:batch].reshape(batch, 6, d)
        w_ext, wq_ext, wkn, wv = _layer_weights(w_in[l], w_uq[l], w_ukv[l])

        qkv, latent = _inproj(x2d, mod3, row(norm1_g[l]), w_ext, seq)
        q_mla, k_mla, v_mla = _mla_proj(latent, pos, invf, row(mla_q_norm_g[l]), row(mla_kv_norm_g[l]),
                                        wq_ext, wkn, wv)
        sb_out = _sb_attn(qkv, batch, seq)
        mla_out = _mla_attn(q_mla, k_mla, v_mla, batch, seq)
        x1, h2 = _mix(sb_out, mla_out, x2d, mod3, row(out_norm_sba_g[l]), row(out_norm_mla_g[l]),
                      row(norm2_g[l]), w_o[l].astype(BF16), seq)
        s1, s2, st = _peer_route(h2, w_peer_q[l].astype(BF16), peer_keys1[l].astype(BF16),
                                 peer_keys2[l].astype(BF16))
        peer = _peer_dense(h2, peer_u[l].astype(BF16), peer_v[l].T.astype(BF16), s1, s2, st)
        x2d = _residual(x1, peer, mod3, row(final_norm_g), seq, final_norm=(l == depth - 1))
    return x2d.reshape(batch, seq, d)
```

```python
import functools
import math

import jax
import jax.numpy as jnp
from jax import lax
from jax.experimental import pallas as pl
from jax.experimental.pallas import tpu as pltpu

F32 = jnp.float32
BF16 = jnp.bfloat16

SB_HEADS = 8
SB_HEAD_DIM = 128
SB_WIDTH = SB_HEADS * SB_HEAD_DIM
MLA_HEADS = 8
MLA_NOPE_DIM = 128
MLA_ROPE_DIM = 64
MLA_V_DIM = 128
MLA_Q_RANK = 512
MLA_KV_RANK = 256
MLA_QK_DIM = MLA_NOPE_DIM + MLA_ROPE_DIM
MLA_WIDTH = MLA_HEADS * MLA_V_DIM
ROPE_THETA = 10000.0
PEER_HEADS = 8
PEER_N_KEYS = 128
PEER_HALF = 128
PEER_TOPK = 16
NORM_EPS = 1e-6
SB_SCALE = SB_HEAD_DIM ** -0.5
MLA_SCALE = MLA_QK_DIM ** -0.5
SB_UNDERFLOW = -105.0

LANES = 128
V7X_VMEM_BYTES = 64 * 1024 * 1024
MLA_HEAD_PAD = 2 * LANES
LATENT_WIDTH = MLA_Q_RANK + MLA_KV_RANK + 2 * LANES
STAT_ROWS = 8


def _tile(n, pref):
    if n <= pref:
        return n
    t = pref
    while n % t:
        t -= 1
    return t


def _params(semantics, vmem_mib):
    return pltpu.CompilerParams(dimension_semantics=semantics,
                                vmem_limit_bytes=min(vmem_mib * 1024 * 1024, V7X_VMEM_BYTES - 4 * 1024 * 1024))


def _rms(x):
    return x * lax.rsqrt(jnp.mean(x * x, axis=-1, keepdims=True) + NORM_EPS)


def _dot(a, b):
    return jnp.dot(a, b, preferred_element_type=F32)


def _dot_nt(a, b):
    return lax.dot_general(a, b, (((1,), (1,)), ((), ())), preferred_element_type=F32)


def _adaln_kernel(c_ref, w_ref, b_ref, o_ref):
    c = c_ref[...]
    c_act = c / (1.0 + jnp.exp(-c))
    o_ref[...] = _dot(c_act.astype(BF16), w_ref[...].astype(BF16)) + b_ref[...]


def _adaln(c_pad, w, b):
    rows, d = c_pad.shape
    n = w.shape[1]
    tn = _tile(n, 1024)
    return pl.pallas_call(
        _adaln_kernel,
        out_shape=jax.ShapeDtypeStruct((rows, n), F32),
        grid=(n // tn,),
        in_specs=[pl.BlockSpec((rows, d), lambda j: (0, 0)),
                  pl.BlockSpec((d, tn), lambda j: (0, j)),
                  pl.BlockSpec((1, tn), lambda j: (0, j))],
        out_specs=pl.BlockSpec((rows, tn), lambda j: (0, j)),
        compiler_params=_params(("arbitrary",), 40),
        name="adaln",
    )(c_pad, w, b)


def _inproj_kernel(x_ref, mod_ref, g_ref, w_ref, oa_ref, ob_ref, h_scr, *, n_head_tiles):
    j = pl.program_id(1)

    @pl.when(j == 0)
    def _():
        y = _rms(x_ref[...]) * g_ref[...]
        h_scr[...] = (y * (1.0 + mod_ref[0, 1:2, :]) + mod_ref[0, 0:1, :]).astype(BF16)

    acc = _dot(h_scr[...], w_ref[...])

    @pl.when(j == 0)
    def _():
        oa_ref[...] = (acc * SB_SCALE).astype(BF16)

    @pl.when(jnp.logical_and(j > 0, j < n_head_tiles))
    def _():
        oa_ref[...] = acc.astype(BF16)

    @pl.when(j == n_head_tiles)
    def _():
        ob_ref[...] = acc


def _inproj(x2d, mod3, g, w_ext, seq):
    t, d = x2d.shape
    tn = SB_WIDTH
    n_head_tiles = 3
    assert w_ext.shape[1] == n_head_tiles * tn + LATENT_WIDTH and LATENT_WIDTH == tn
    tm = _tile(seq, 512)
    return pl.pallas_call(
        functools.partial(_inproj_kernel, n_head_tiles=n_head_tiles),
        out_shape=(jax.ShapeDtypeStruct((t, n_head_tiles * tn), BF16),
                   jax.ShapeDtypeStruct((t, LATENT_WIDTH), F32)),
        grid=(t // tm, n_head_tiles + 1),
        in_specs=[pl.BlockSpec((tm, d), lambda i, j: (i, 0)),
                  pl.BlockSpec((1, 6, d), lambda i, j: ((i * tm) // seq, 0, 0)),
                  pl.BlockSpec((1, d), lambda i, j: (0, 0)),
                  pl.BlockSpec((d, tn), lambda i, j: (0, j))],
        out_specs=(pl.BlockSpec((tm, tn), lambda i, j: (i, jnp.minimum(j, n_head_tiles - 1))),
                   pl.BlockSpec((tm, LATENT_WIDTH), lambda i, j: (i, 0))),
        scratch_shapes=[pltpu.VMEM((tm, d), BF16)],
        compiler_params=_params(("parallel", "arbitrary"), 48),
        name="inproj",
    )(x2d, mod3, g, w_ext)


def _mla_proj_kernel(c_ref, pos_ref, invf_ref, gq_ref, gkv_ref, wq_ref, wkn_ref, wv_ref,
                     q_ref, k_ref, v_ref):
    cqn = (_rms(c_ref[:, :MLA_Q_RANK]) * gq_ref[...]).astype(BF16)
    ckvn = (_rms(c_ref[:, MLA_Q_RANK:MLA_Q_RANK + MLA_KV_RANK]) * gkv_ref[...]).astype(BF16)
    kr = c_ref[:, MLA_Q_RANK + MLA_KV_RANK:MLA_Q_RANK + MLA_KV_RANK + LANES]
    kr_rot = c_ref[:, MLA_Q_RANK + MLA_KV_RANK + LANES:]
    ang = pos_ref[...] * invf_ref[...]
    cos = jnp.cos(ang)
    sin = jnp.sin(ang)
    k_pe = (kr * cos + kr_rot * sin).astype(BF16)
    qa = _dot(cqn, wq_ref[...])
    kn = _dot(ckvn, wkn_ref[...])
    v_ref[...] = _dot(ckvn, wv_ref[...]).astype(BF16)
    for h in range(MLA_HEADS):
        b = 3 * LANES * h
        o = MLA_HEAD_PAD * h
        q_ref[:, o:o + LANES] = (qa[:, b:b + LANES] * MLA_SCALE).astype(BF16)
        q_pe = qa[:, b + LANES:b + 2 * LANES] * cos + qa[:, b + 2 * LANES:b + 3 * LANES] * sin
        q_ref[:, o + LANES:o + 2 * LANES] = (q_pe * MLA_SCALE).astype(BF16)
        k_ref[:, o:o + LANES] = kn[:, h * LANES:(h + 1) * LANES].astype(BF16)
        k_ref[:, o + LANES:o + 2 * LANES] = k_pe


def _mla_proj(latent, pos, invf, gq, gkv, wq, wkn, wv):
    t = latent.shape[0]
    tm = _tile(t, 512)
    full = lambda a: pl.BlockSpec(a.shape, lambda i: (0,) * a.ndim)
    return pl.pallas_call(
        _mla_proj_kernel,
        out_shape=(jax.ShapeDtypeStruct((t, MLA_HEADS * MLA_HEAD_PAD), BF16),
                   jax.ShapeDtypeStruct((t, MLA_HEADS * MLA_HEAD_PAD), BF16),
                   jax.ShapeDtypeStruct((t, MLA_WIDTH), BF16)),
        grid=(t // tm,),
        in_specs=[pl.BlockSpec((tm, LATENT_WIDTH), lambda i: (i, 0)),
                  pl.BlockSpec((tm, 1), lambda i: (i, 0)),
                  full(invf), full(gq), full(gkv), full(wq), full(wkn), full(wv)],
        out_specs=(pl.BlockSpec((tm, MLA_HEADS * MLA_HEAD_PAD), lambda i: (i, 0)),
                   pl.BlockSpec((tm, MLA_HEADS * MLA_HEAD_PAD), lambda i: (i, 0)),
                   pl.BlockSpec((tm, MLA_WIDTH), lambda i: (i, 0))),
        compiler_params=_params(("parallel",), 48),
        name="mla_proj",
    )(latent, pos, invf, gq, gkv, wq, wkn, wv)


def _sb_attn_kernel(q_ref, k_ref, v_ref, o_ref, *, tb, heads):
    qi = pl.program_id(1)
    row = lax.broadcasted_iota(jnp.int32, (tb, tb), 0)
    col = lax.broadcasted_iota(jnp.int32, (tb, tb), 1)
    later = (row > col).astype(BF16)
    causal = col < row

    def head_block(h, kb, acc, run, masked):
        lanes = slice(h * SB_HEAD_DIM, (h + 1) * SB_HEAD_DIM)
        start = pl.multiple_of(kb * tb, tb)
        z = _dot_nt(q_ref[:, lanes], k_ref[pl.ds(start, tb), lanes])
        t = jnp.log(1.0 + jnp.exp(-jnp.abs(z)))
        log_1m = -(jnp.maximum(z, 0.0) + t)
        log_b = jnp.minimum(z, 0.0) - t
        if masked:
            log_1m = jnp.where(causal, log_1m, 0.0)
        hi = log_1m.astype(BF16)
        r1 = log_1m - hi.astype(F32)
        mid = r1.astype(BF16)
        lo = (r1 - mid.astype(F32)).astype(BF16)
        sums = _dot(jnp.concatenate([hi, mid, lo], axis=0), later)
        between = sums[:tb] + sums[tb:2 * tb] + sums[2 * tb:]
        w = jnp.exp(log_b + between + run)
        if masked:
            w = jnp.where(causal, w, 0.0)
        acc = acc + _dot(w.astype(BF16), v_ref[pl.ds(start, tb), lanes])
        run = run + jnp.sum(log_1m, axis=1, keepdims=True)
        return acc, run

    def block(kb, carry, masked):
        return tuple(head_block(h, kb, *carry[h], masked) for h in range(heads))

    carry = tuple((jnp.zeros((tb, SB_HEAD_DIM), F32), jnp.zeros((tb, 1), F32)) for _ in range(heads))
    carry = block(qi, carry, True)

    def live(state):
        i, carry = state
        slowest = functools.reduce(jnp.maximum, [jnp.max(c[1]) for c in carry])
        return jnp.logical_and(i < qi, slowest > SB_UNDERFLOW)

    def step(state):
        i, carry = state
        return i + 1, block(qi - 1 - i, carry, False)

    _, carry = lax.while_loop(live, step, (jnp.int32(0), carry))
    for h in range(heads):
        o_ref[:, h * SB_HEAD_DIM:(h + 1) * SB_HEAD_DIM] = carry[h][0]


def _sb_attn(qkv, batch, seq):
    t = qkv.shape[0]
    tb = _tile(seq, 256)
    nq = seq // tb
    heads = 2
    groups = SB_HEADS // heads
    width = heads * SB_HEAD_DIM
    return pl.pallas_call(
        functools.partial(_sb_attn_kernel, tb=tb, heads=heads),
        out_shape=jax.ShapeDtypeStruct((t, SB_WIDTH), F32),
        grid=(batch * groups, nq),
        in_specs=[pl.BlockSpec((tb, width), lambda g, i: ((g // groups) * nq + i, g % groups)),
                  pl.BlockSpec((seq, width), lambda g, i: (g // groups, groups + g % groups)),
                  pl.BlockSpec((seq, width), lambda g, i: (g // groups, 2 * groups + g % groups))],
        out_specs=pl.BlockSpec((tb, width), lambda g, i: ((g // groups) * nq + i, g % groups)),
        compiler_params=_params(("parallel", "arbitrary"), 32),
        name="sb_attn",
    )(qkv, qkv, qkv)


def _mla_attn_kernel(q_ref, k_ref, v_ref, o_ref, *, tb, heads):
    qi = pl.program_id(1)
    row = lax.broadcasted_iota(jnp.int32, (tb, tb), 0)
    col = lax.broadcasted_iota(jnp.int32, (tb, tb), 1)
    causal = col <= row

    def head_block(h, kb, acc, m, l, masked):
        start = pl.multiple_of(kb * tb, tb)
        s = _dot_nt(q_ref[:, h * MLA_HEAD_PAD:(h + 1) * MLA_HEAD_PAD],
                    k_ref[pl.ds(start, tb), h * MLA_HEAD_PAD:(h + 1) * MLA_HEAD_PAD])
        if masked:
            s = jnp.where(causal, s, -jnp.inf)
        m_new = jnp.maximum(m, jnp.max(s, axis=1, keepdims=True))
        alpha = jnp.exp(m - m_new)
        p = jnp.exp(s - m_new)
        l = alpha * l + jnp.sum(p, axis=1, keepdims=True)
        acc = alpha * acc + _dot(p.astype(BF16), v_ref[pl.ds(start, tb), h * MLA_V_DIM:(h + 1) * MLA_V_DIM])
        return acc, m_new, l

    def block(kb, carry, masked):
        return tuple(head_block(h, kb, *carry[h], masked) for h in range(heads))

    carry = tuple((jnp.zeros((tb, MLA_V_DIM), F32), jnp.full((tb, 1), -jnp.inf, F32), jnp.zeros((tb, 1), F32))
                  for _ in range(heads))
    carry = block(qi, carry, True)
    carry = lax.fori_loop(0, qi, lambda i, c: block(qi - 1 - i, c, False), carry)
    for h in range(heads):
        acc, _, l = carry[h]
        o_ref[:, h * MLA_V_DIM:(h + 1) * MLA_V_DIM] = acc / l


def _mla_attn(q, k, v, batch, seq):
    t = q.shape[0]
    tb = _tile(seq, 512)
    nq = seq // tb
    heads = 2
    groups = MLA_HEADS // heads
    return pl.pallas_call(
        functools.partial(_mla_attn_kernel, tb=tb, heads=heads),
        out_shape=jax.ShapeDtypeStruct((t, MLA_WIDTH), F32),
        grid=(batch * groups, nq),
        in_specs=[pl.BlockSpec((tb, heads * MLA_HEAD_PAD), lambda g, i: ((g // groups) * nq + i, g % groups)),
                  pl.BlockSpec((seq, heads * MLA_HEAD_PAD), lambda g, i: (g // groups, g % groups)),
                  pl.BlockSpec((seq, heads * MLA_V_DIM), lambda g, i: (g // groups, g % groups))],
        out_specs=pl.BlockSpec((tb, heads * MLA_V_DIM), lambda g, i: ((g // groups) * nq + i, g % groups)),
        compiler_params=_params(("parallel", "arbitrary"), 40),
        name="mla_attn",
    )(q, k, v)


def _mix_kernel(sb_ref, ml_ref, x_ref, mod_ref, gsb_ref, gml_ref, g2_ref, wo_ref, x1_ref, h2_ref):
    a = (_rms(sb_ref[...]) * gsb_ref[...]).astype(BF16)
    b = (_rms(ml_ref[...]) * gml_ref[...]).astype(BF16)
    mixed = _dot(a, wo_ref[:SB_WIDTH, :]) + _dot(b, wo_ref[SB_WIDTH:, :])
    x1 = x_ref[...] + mod_ref[0, 2:3, :] * mixed
    x1_ref[...] = x1
    h2 = _rms(x1) * g2_ref[...]
    h2_ref[...] = (h2 * (1.0 + mod_ref[0, 4:5, :]) + mod_ref[0, 3:4, :]).astype(BF16)


def _mix(sb, ml, x2d, mod3, gsb, gml, g2, wo, seq):
    t, d = x2d.shape
    tm = _tile(seq, 256)
    full = lambda a: pl.BlockSpec(a.shape, lambda i: (0,) * a.ndim)
    return pl.pallas_call(
        _mix_kernel,
        out_shape=(jax.ShapeDtypeStruct((t, d), F32), jax.ShapeDtypeStruct((t, d), BF16)),
        grid=(t // tm,),
        in_specs=[pl.BlockSpec((tm, SB_WIDTH), lambda i: (i, 0)),
                  pl.BlockSpec((tm, MLA_WIDTH), lambda i: (i, 0)),
                  pl.BlockSpec((tm, d), lambda i: (i, 0)),
                  pl.BlockSpec((1, 6, d), lambda i: ((i * tm) // seq, 0, 0)),
                  full(gsb), full(gml), full(g2), full(wo)],
        out_specs=(pl.BlockSpec((tm, d), lambda i: (i, 0)), pl.BlockSpec((tm, d), lambda i: (i, 0))),
        compiler_params=_params(("parallel",), 48),
        name="mix",
    )(sb, ml, x2d, mod3, gsb, gml, g2, wo)


def _top_values(s, k):
    vals = []
    for _ in range(k):
        mx = jnp.max(s, axis=0, keepdims=True)
        vals.append(mx)
        s = jnp.where(s == mx, -jnp.inf, s)
    return vals


def _peer_route_kernel(h2_ref, wq_ref, k1_ref, k2_ref, s1_ref, s2_ref, st_ref):
    qp = _dot(h2_ref[...], wq_ref[...]).astype(BF16)
    tm = qp.shape[0]
    for h in range(PEER_HEADS):
        o = 2 * PEER_HALF * h
        s1 = _dot_nt(k1_ref[h], qp[:, o:o + PEER_HALF])
        s2 = _dot_nt(k2_ref[h], qp[:, o + PEER_HALF:o + 2 * PEER_HALF])
        s1_ref[h] = s1
        s2_ref[h] = s2
        v1 = _top_values(s1, PEER_TOPK)
        v2 = _top_values(s2, PEER_TOPK)
        cand = jnp.concatenate([v1[a] + v2[b] for a in range(PEER_TOPK) for b in range(PEER_TOPK)
                                if (a + 1) * (b + 1) <= PEER_TOPK], axis=0)
        best = _top_values(cand, PEER_TOPK)
        z = best[0] * 0.0
        for c in best:
            z = z + jnp.exp(c - best[0])
        st_ref[h] = jnp.concatenate([best[-1], v1[0], v2[0], 1.0 / z,
                                     jnp.zeros((STAT_ROWS - 4, tm), F32)], axis=0)


def _peer_route(h2, wq, k1, k2):
    t, d = h2.shape
    tm = _tile(t, 256)
    full = lambda a: pl.BlockSpec(a.shape, lambda i: (0,) * a.ndim)
    return pl.pallas_call(
        _peer_route_kernel,
        out_shape=(jax.ShapeDtypeStruct((PEER_HEADS, PEER_N_KEYS, t), F32),
                   jax.ShapeDtypeStruct((PEER_HEADS, PEER_N_KEYS, t), F32),
                   jax.ShapeDtypeStruct((PEER_HEADS, STAT_ROWS, t), F32)),
        grid=(t // tm,),
        in_specs=[pl.BlockSpec((tm, d), lambda i: (i, 0)), full(wq), full(k1), full(k2)],
        out_specs=(pl.BlockSpec((PEER_HEADS, PEER_N_KEYS, tm), lambda i: (0, 0, i)),
                   pl.BlockSpec((PEER_HEADS, PEER_N_KEYS, tm), lambda i: (0, 0, i)),
                   pl.BlockSpec((PEER_HEADS, STAT_ROWS, tm), lambda i: (0, 0, i))),
        compiler_params=_params(("parallel",), 48),
        name="peer_route",
    )(h2, wq, k1, k2)


def _peer_dense_kernel(h2_ref, u_ref, vt_ref, s1_ref, s2_ref, st_ref, o_ref, e1_ref, e2_ref, *, rows):
    j = pl.program_id(1)

    @pl.when(j == 0)
    def _():
        o_ref[...] = jnp.zeros_like(o_ref)
        for h in range(PEER_HEADS):
            e1_ref[h] = jnp.exp(s1_ref[h] - st_ref[h, 1:2, :])
            e2_ref[h] = jnp.exp(s2_ref[h] - st_ref[h, 2:3, :]) * st_ref[h, 3:4, :]

    pre = _dot_nt(u_ref[...], h2_ref[...])
    act = 0.5 * pre * (1.0 + lax.erf(pre * (2.0 ** -0.5)))
    gates = []
    for r in range(rows):
        i1 = j * rows + r
        g = None
        for h in range(PEER_HEADS):
            pair = s2_ref[h] + s1_ref[h, pl.ds(i1, 1), :]
            term = jnp.where(pair >= st_ref[h, 0:1, :], e2_ref[h], 0.0) * e1_ref[h, pl.ds(i1, 1), :]
            g = term if g is None else g + term
        gates.append(g)
    gate = jnp.concatenate(gates, axis=0)
    o_ref[...] += _dot(vt_ref[...], (act * gate).astype(BF16))


def _peer_dense(h2, u, vt, s1, s2, st):
    t, d = h2.shape
    n_exp = u.shape[0]
    tm = _tile(t, 512)
    te = 512
    return pl.pallas_call(
        functools.partial(_peer_dense_kernel, rows=te // PEER_N_KEYS),
        out_shape=jax.ShapeDtypeStruct((d, t), F32),
        grid=(t // tm, n_exp // te),
        in_specs=[pl.BlockSpec((tm, d), lambda i, j: (i, 0)),
                  pl.BlockSpec((te, d), lambda i, j: (j, 0)),
                  pl.BlockSpec((d, te), lambda i, j: (0, j)),
                  pl.BlockSpec((PEER_HEADS, PEER_N_KEYS, tm), lambda i, j: (0, 0, i)),
                  pl.BlockSpec((PEER_HEADS, PEER_N_KEYS, tm), lambda i, j: (0, 0, i)),
                  pl.BlockSpec((PEER_HEADS, STAT_ROWS, tm), lambda i, j: (0, 0, i))],
        out_specs=pl.BlockSpec((d, tm), lambda i, j: (0, i)),
        scratch_shapes=[pltpu.VMEM((PEER_HEADS, PEER_N_KEYS, tm), F32),
                        pltpu.VMEM((PEER_HEADS, PEER_N_KEYS, tm), F32)],
        compiler_params=_params(("parallel", "arbitrary"), 56),
        name="peer_dense",
    )(h2, u, vt, s1, s2, st)


def _residual_kernel(x1_ref, pt_ref, mod_ref, gf_ref, o_ref, *, final_norm):
    x2 = x1_ref[...] + mod_ref[0, 5:6, :] * pt_ref[...].T
    o_ref[...] = _rms(x2) * gf_ref[...] if final_norm else x2


def _residual(x1, peer_t, mod3, gf, seq, final_norm):
    t, d = x1.shape
    tm = _tile(seq, 512)
    return pl.pallas_call(
        functools.partial(_residual_kernel, final_norm=final_norm),
        out_shape=jax.ShapeDtypeStruct((t, d), F32),
        grid=(t // tm,),
        in_specs=[pl.BlockSpec((tm, d), lambda i: (i, 0)),
                  pl.BlockSpec((d, tm), lambda i: (0, i)),
                  pl.BlockSpec((1, 6, d), lambda i: ((i * tm) // seq, 0, 0)),
                  pl.BlockSpec((1, d), lambda i: (0, 0))],
        out_specs=pl.BlockSpec((tm, d), lambda i: (i, 0)),
        compiler_params=_params(("parallel",), 40),
        name="residual",
    )(x1, peer_t, mod3, gf)


def _rot_half_cols(w):
    half = w.shape[-1] // 2
    return jnp.concatenate([-w[..., half:], w[..., :half]], axis=-1)


def _pad_cols(w, width):
    return jnp.concatenate([w, jnp.zeros(w.shape[:-1] + (width - w.shape[-1],), w.dtype)], axis=-1)


def _layer_weights(w_in, w_uq, w_ukv):
    d = w_in.shape[0]
    n_heads_cols = 3 * SB_WIDTH
    w_kr = w_in[:, n_heads_cols + MLA_Q_RANK + MLA_KV_RANK:]
    w_ext = jnp.concatenate([w_in[:, :n_heads_cols + MLA_Q_RANK + MLA_KV_RANK],
                             _pad_cols(w_kr, LANES), _pad_cols(_rot_half_cols(w_kr), LANES)], axis=1)
    wq = w_uq.reshape(MLA_Q_RANK, MLA_HEADS, MLA_QK_DIM)
    wq_pe = wq[..., MLA_NOPE_DIM:]
    wq_ext = jnp.concatenate([wq[..., :MLA_NOPE_DIM], _pad_cols(wq_pe, LANES),
                              _pad_cols(_rot_half_cols(wq_pe), LANES)], axis=-1)
    wq_ext = wq_ext.reshape(MLA_Q_RANK, MLA_HEADS * 3 * LANES)
    wkv = w_ukv.reshape(MLA_KV_RANK, MLA_HEADS, MLA_NOPE_DIM + MLA_V_DIM)
    wkn = wkv[..., :MLA_NOPE_DIM].reshape(MLA_KV_RANK, MLA_HEADS * MLA_NOPE_DIM)
    wv = wkv[..., MLA_NOPE_DIM:].reshape(MLA_KV_RANK, MLA_WIDTH)
    del d
    return w_ext.astype(BF16), wq_ext.astype(BF16), wkn.astype(BF16), wv.astype(BF16)


def kernel(x, c, positions, w_ada, b_ada, norm1_g, w_in, mla_q_norm_g, w_uq, mla_kv_norm_g, w_ukv,
           out_norm_sba_g, out_norm_mla_g, w_o, norm2_g, w_peer_q, peer_keys1, peer_keys2,
           peer_u, peer_v, final_norm_g):
    batch, seq, d = x.shape
    depth = w_ada.shape[0]
    t = batch * seq
    x2d = x.reshape(t, d)
    pos = positions.reshape(t, 1).astype(F32)
    inv_freq = ROPE_THETA ** (-jnp.arange(0, MLA_ROPE_DIM, 2, dtype=F32) / MLA_ROPE_DIM)
    invf = _pad_cols(jnp.concatenate([inv_freq, inv_freq])[None, :], LANES)
    c_rows = 16
    c_pad = jnp.concatenate([c, jnp.zeros((c_rows - batch, d), c.dtype)], axis=0)
    row = lambda g: g[None, :]

    for l in range(depth):
        mod = _adaln(c_pad, w_ada[l], b_ada[l][None, :])
        mod3 = mod[:batch].reshape(batch, 6, d)
        w_ext, wq_ext, wkn, wv = _layer_weights(w_in[l], w_uq[l], w_ukv[l])

        qkv, latent = _inproj(x2d, mod3, row(norm1_g[l]), w_ext, seq)
        q_mla, k_mla, v_mla = _mla_proj(latent, pos, invf, row(mla_q_norm_g[l]), row(mla_kv_norm_g[l]),
                                        wq_ext, wkn, wv)
        sb_out = _sb_attn(qkv, batch, seq)
        mla_out = _mla_attn(q_mla, k_mla, v_mla, batch, seq)
        x1, h2 = _mix(sb_out, mla_out, x2d, mod3, row(out_norm_sba_g[l]), row(out_norm_mla_g[l]),
                      row(norm2_g[l]), w_o[l].astype(BF16), seq)
        s1, s2, st = _peer_route(h2, w_peer_q[l].astype(BF16), peer_keys1[l].astype(BF16),
                                 peer_keys2[l].astype(BF16))
        peer = _peer_dense(h2, peer_u[l].astype(BF16), peer_v[l].T.astype(BF16), s1, s2, st)
        x2d = _residual(x1, peer, mod3, row(final_norm_g), seq, final_norm=(l == depth - 1))
    return x2d.reshape(batch, seq, d)
```

```python
import functools

import jax
import jax.numpy as jnp
from jax import lax
from jax.experimental import pallas as pl
from jax.experimental.pallas import tpu as pltpu

F32 = jnp.float32
BF16 = jnp.bfloat16

SB_HEADS = 8
SB_HEAD_DIM = 128
SB_WIDTH = SB_HEADS * SB_HEAD_DIM
MLA_HEADS = 8
MLA_NOPE_DIM = 128
MLA_ROPE_DIM = 64
MLA_V_DIM = 128
MLA_Q_RANK = 512
MLA_KV_RANK = 256
MLA_QK_DIM = MLA_NOPE_DIM + MLA_ROPE_DIM
MLA_WIDTH = MLA_HEADS * MLA_V_DIM
ROPE_THETA = 10000.0
PEER_HEADS = 8
PEER_N_KEYS = 128
PEER_HALF = 128
PEER_TOPK = 16
NORM_EPS = 1e-6
SB_SCALE = SB_HEAD_DIM ** -0.5
MLA_SCALE = MLA_QK_DIM ** -0.5
SB_UNDERFLOW = -105.0

LANES = 128
MXU_WIDTH = 256
V7X_VMEM_BYTES = 64 * 1024 * 1024
MLA_HEAD_PAD = 2 * LANES
LATENT_WIDTH = MLA_Q_RANK + MLA_KV_RANK + 2 * LANES
SUBLANES = 8
BF16_ROWS = 16
NOT_SELECTED = 127.0


def _tile(n, pref):
    if n <= pref:
        return n
    t = pref
    while n % t:
        t -= 1
    return t


def _params(semantics, vmem_mib):
    return pltpu.CompilerParams(dimension_semantics=semantics,
                                vmem_limit_bytes=min(vmem_mib * 1024 * 1024, V7X_VMEM_BYTES - 4 * 1024 * 1024))


def _rms(x):
    return x * lax.rsqrt(jnp.mean(x * x, axis=-1, keepdims=True) + NORM_EPS)


def _dot(a, b):
    return jnp.dot(a, b, preferred_element_type=F32)


def _emit_alternately(first, second):
    n, m = len(first), len(second)
    i = k = 0
    while i < n or k < m:
        if k >= m or (i < n and i * m <= k * n):
            first[i]()
            i += 1
        else:
            second[k]()
            k += 1


def _dot_nt(a, b):
    return lax.dot_general(a, b, (((1,), (1,)), ((), ())), preferred_element_type=F32)


def _adaln_kernel(c_ref, w_ref, b_ref, o_ref):
    c = c_ref[...]
    c_act = c / (1.0 + jnp.exp(-c))
    o_ref[...] = _dot(c_act.astype(BF16), w_ref[...].astype(BF16)) + b_ref[...]


def _adaln(c_pad, w, b):
    rows, d = c_pad.shape
    n = w.shape[1]
    tn = _tile(n, 1024)
    return pl.pallas_call(
        _adaln_kernel,
        out_shape=jax.ShapeDtypeStruct((rows, n), F32),
        grid=(n // tn,),
        in_specs=[pl.BlockSpec((rows, d), lambda j: (0, 0)),
                  pl.BlockSpec((d, tn), lambda j: (0, j)),
                  pl.BlockSpec((1, tn), lambda j: (0, j))],
        out_specs=pl.BlockSpec((rows, tn), lambda j: (0, j)),
        compiler_params=_params(("arbitrary",), 40),
        name="adaln",
    )(c_pad, w, b)


def _inproj_kernel(x_ref, mod_ref, g_ref, wh_ref, wl_ref, oa_ref, ob_ref, h_scr, *, n_head_tiles):
    j = pl.program_id(1)

    @pl.when(j == 0)
    def _():
        y = _rms(x_ref[...]) * g_ref[...]
        h = (y * (1.0 + mod_ref[0, 1:2, :]) + mod_ref[0, 0:1, :]).astype(BF16)
        h_scr[...] = h
        oa_ref[...] = (_dot(h, wh_ref[...]) * SB_SCALE).astype(BF16)

    @pl.when(jnp.logical_and(j > 0, j < n_head_tiles))
    def _():
        oa_ref[...] = _dot(h_scr[...], wh_ref[...]).astype(BF16)

    @pl.when(j == n_head_tiles)
    def _():
        ob_ref[...] = _dot(h_scr[...], wl_ref[...])


def _inproj(x2d, mod3, g, w_heads, w_latent, seq):
    t, d = x2d.shape
    tn = SB_WIDTH
    n_head_tiles = 3
    assert w_latent.shape[1] == LATENT_WIDTH
    tm = _tile(seq, 512)
    return pl.pallas_call(
        functools.partial(_inproj_kernel, n_head_tiles=n_head_tiles),
        out_shape=(jax.ShapeDtypeStruct((t, n_head_tiles * tn), BF16),
                   jax.ShapeDtypeStruct((t, LATENT_WIDTH), F32)),
        grid=(t // tm, n_head_tiles + 1),
        in_specs=[pl.BlockSpec((tm, d), lambda i, j: (i, 0)),
                  pl.BlockSpec((1, 6, d), lambda i, j: ((i * tm) // seq, 0, 0)),
                  pl.BlockSpec((1, d), lambda i, j: (0, 0)),
                  pl.BlockSpec((d, tn), lambda i, j: (0, jnp.minimum(j, n_head_tiles - 1))),
                  pl.BlockSpec((d, LATENT_WIDTH), lambda i, j: (0, 0))],
        out_specs=(pl.BlockSpec((tm, tn), lambda i, j: (i, jnp.minimum(j, n_head_tiles - 1))),
                   pl.BlockSpec((tm, LATENT_WIDTH), lambda i, j: (i, 0))),
        scratch_shapes=[pltpu.VMEM((tm, d), BF16)],
        compiler_params=_params(("parallel", "arbitrary"), 48),
        name="inproj",
    )(x2d, mod3, g, w_heads, w_latent)


def _mla_proj_kernel(c_ref, pos_ref, invf_ref, gq_ref, gkv_ref, wq_ref, wkn_ref, wv_ref,
                     q_ref, k_ref, v_ref):
    cqn = (_rms(c_ref[:, :MLA_Q_RANK]) * gq_ref[...]).astype(BF16)
    ckvn = (_rms(c_ref[:, MLA_Q_RANK:MLA_Q_RANK + MLA_KV_RANK]) * gkv_ref[...]).astype(BF16)
    kr = c_ref[:, MLA_Q_RANK + MLA_KV_RANK:MLA_Q_RANK + MLA_KV_RANK + LANES]
    kr_rot = c_ref[:, MLA_Q_RANK + MLA_KV_RANK + LANES:]
    ang = pos_ref[...] * invf_ref[...]
    cos = jnp.cos(ang)
    sin = jnp.sin(ang)
    k_pe = (kr * cos + kr_rot * sin).astype(BF16)
    qa = _dot(cqn, wq_ref[...])
    kn = _dot(ckvn, wkn_ref[...])
    v_ref[...] = _dot(ckvn, wv_ref[...]).astype(BF16)
    for h in range(MLA_HEADS):
        b = 3 * LANES * h
        o = MLA_HEAD_PAD * h
        q_ref[:, o:o + LANES] = (qa[:, b:b + LANES] * MLA_SCALE).astype(BF16)
        q_pe = qa[:, b + LANES:b + 2 * LANES] * cos + qa[:, b + 2 * LANES:b + 3 * LANES] * sin
        q_ref[:, o + LANES:o + 2 * LANES] = (q_pe * MLA_SCALE).astype(BF16)
        k_ref[:, o:o + LANES] = kn[:, h * LANES:(h + 1) * LANES].astype(BF16)
        k_ref[:, o + LANES:o + 2 * LANES] = k_pe


def _mla_proj(latent, pos, invf, gq, gkv, wq, wkn, wv):
    t = latent.shape[0]
    tm = _tile(t, 512)
    full = lambda a: pl.BlockSpec(a.shape, lambda i: (0,) * a.ndim)
    return pl.pallas_call(
        _mla_proj_kernel,
        out_shape=(jax.ShapeDtypeStruct((t, MLA_HEADS * MLA_HEAD_PAD), BF16),
                   jax.ShapeDtypeStruct((t, MLA_HEADS * MLA_HEAD_PAD), BF16),
                   jax.ShapeDtypeStruct((t, MLA_WIDTH), BF16)),
        grid=(t // tm,),
        in_specs=[pl.BlockSpec((tm, LATENT_WIDTH), lambda i: (i, 0)),
                  pl.BlockSpec((tm, 1), lambda i: (i, 0)),
                  full(invf), full(gq), full(gkv), full(wq), full(wkn), full(wv)],
        out_specs=(pl.BlockSpec((tm, MLA_HEADS * MLA_HEAD_PAD), lambda i: (i, 0)),
                   pl.BlockSpec((tm, MLA_HEADS * MLA_HEAD_PAD), lambda i: (i, 0)),
                   pl.BlockSpec((tm, MLA_WIDTH), lambda i: (i, 0))),
        compiler_params=_params(("parallel",), 48),
        name="mla_proj",
    )(latent, pos, invf, gq, gkv, wq, wkn, wv)


def _sb_attn_kernel(q_ref, k_ref, v_ref, o_ref, *, tq, tk, heads):
    qi = pl.program_id(1)
    n_diag = tq // tk
    key_j = lax.broadcasted_iota(jnp.int32, (tk, tk), 0)
    key_s = lax.broadcasted_iota(jnp.int32, (tk, tk), 1)
    later = (key_j > key_s).astype(BF16)
    row = lax.broadcasted_iota(jnp.int32, (tq, tk), 0)
    col = lax.broadcasted_iota(jnp.int32, (tq, tk), 1)

    def head_block(h, kb, acc, run, diag):
        lanes = slice(h * SB_HEAD_DIM, (h + 1) * SB_HEAD_DIM)
        start = pl.multiple_of(kb * tk, tk)
        z = _dot_nt(q_ref[:, lanes], k_ref[pl.ds(start, tk), lanes])
        t = jnp.log(1.0 + jnp.exp(-jnp.abs(z)))
        log_1m = -(jnp.maximum(z, 0.0) + t)
        log_b = jnp.minimum(z, 0.0) - t
        if diag is not None:
            causal = col + diag * tk < row
            log_1m = jnp.where(causal, log_1m, 0.0)
        hi = log_1m.astype(BF16)
        lo = (log_1m - hi.astype(F32)).astype(BF16)
        sums = _dot(jnp.concatenate([hi, lo], axis=0), later)
        between = sums[:tq] + sums[tq:]
        w = jnp.exp(log_b + between + run)
        if diag is not None:
            w = jnp.where(causal, w, 0.0)
        acc = acc + _dot(w.astype(BF16), v_ref[pl.ds(start, tk), lanes])
        run = run + jnp.sum(log_1m, axis=1, keepdims=True)
        return acc, run

    def block(kb, carry, diag=None):
        return tuple(head_block(h, kb, *carry[h], diag) for h in range(heads))

    carry = tuple((jnp.zeros((tq, SB_HEAD_DIM), F32), jnp.zeros((tq, 1), F32)) for _ in range(heads))
    for d in reversed(range(n_diag)):
        carry = block(qi * n_diag + d, carry, d)
    n_below = qi * n_diag

    def live(state):
        i, carry = state
        slowest = functools.reduce(jnp.maximum, [jnp.max(c[1]) for c in carry])
        return jnp.logical_and(i < n_below, slowest > SB_UNDERFLOW)

    def step(state):
        i, carry = state
        return i + 1, block(n_below - 1 - i, carry)

    _, carry = lax.while_loop(live, step, (jnp.int32(0), carry))
    for h in range(heads):
        o_ref[:, h * SB_HEAD_DIM:(h + 1) * SB_HEAD_DIM] = carry[h][0]


def _sb_attn(qkv, batch, seq):
    t = qkv.shape[0]
    tq = _tile(seq, 512)
    tk = _tile(tq, 256)
    nq = seq // tq
    heads = 2
    groups = SB_HEADS // heads
    width = heads * SB_HEAD_DIM
    return pl.pallas_call(
        functools.partial(_sb_attn_kernel, tq=tq, tk=tk, heads=heads),
        out_shape=jax.ShapeDtypeStruct((t, SB_WIDTH), F32),
        grid=(batch * groups, nq),
        in_specs=[pl.BlockSpec((tq, width), lambda g, i: ((g // groups) * nq + i, g % groups)),
                  pl.BlockSpec((seq, width), lambda g, i: (g // groups, groups + g % groups)),
                  pl.BlockSpec((seq, width), lambda g, i: (g // groups, 2 * groups + g % groups))],
        out_specs=pl.BlockSpec((tq, width), lambda g, i: ((g // groups) * nq + i, g % groups)),
        compiler_params=_params(("parallel", "arbitrary"), 32),
        name="sb_attn",
    )(qkv, qkv, qkv)


def _mla_attn_kernel(q_ref, k_ref, v_ref, o_ref, *, tq, tk, heads):
    qi = pl.program_id(1)
    n_diag = tq // tk
    row = lax.broadcasted_iota(jnp.int32, (tq, tk), 0)
    col = lax.broadcasted_iota(jnp.int32, (tq, tk), 1)

    def head_block(h, kb, acc, m, l, diag):
        start = pl.multiple_of(kb * tk, tk)
        s = _dot_nt(q_ref[:, h * MLA_HEAD_PAD:(h + 1) * MLA_HEAD_PAD],
                    k_ref[pl.ds(start, tk), h * MLA_HEAD_PAD:(h + 1) * MLA_HEAD_PAD])
        if diag is not None:
            s = jnp.where(col + diag * tk <= row, s, -jnp.inf)
        m_new = jnp.maximum(m, jnp.max(s, axis=1, keepdims=True))
        alpha = jnp.exp(m - m_new)
        p = jnp.exp(s - m_new)
        l = alpha * l + jnp.sum(p, axis=1, keepdims=True)
        acc = alpha * acc + _dot(p.astype(BF16), v_ref[pl.ds(start, tk), h * MLA_V_DIM:(h + 1) * MLA_V_DIM])
        return acc, m_new, l

    def block(kb, carry, diag=None):
        return tuple(head_block(h, kb, *carry[h], diag) for h in range(heads))

    carry = tuple((jnp.zeros((tq, MLA_V_DIM), F32), jnp.full((tq, 1), -jnp.inf, F32), jnp.zeros((tq, 1), F32))
                  for _ in range(heads))
    for d in range(n_diag):
        carry = block(qi * n_diag + d, carry, d)
    carry = lax.fori_loop(0, qi * n_diag, lambda i, c: block(qi * n_diag - 1 - i, c), carry)
    for h in range(heads):
        acc, _, l = carry[h]
        o_ref[:, h * MLA_V_DIM:(h + 1) * MLA_V_DIM] = acc / l


def _mla_attn(q, k, v, batch, seq):
    t = q.shape[0]
    tq = _tile(seq, 512)
    tk = tq
    nq = seq // tq
    heads = 2
    groups = MLA_HEADS // heads
    return pl.pallas_call(
        functools.partial(_mla_attn_kernel, tq=tq, tk=tk, heads=heads),
        out_shape=jax.ShapeDtypeStruct((t, MLA_WIDTH), F32),
        grid=(batch * groups, nq),
        in_specs=[pl.BlockSpec((tq, heads * MLA_HEAD_PAD), lambda g, i: ((g // groups) * nq + i, g % groups)),
                  pl.BlockSpec((seq, heads * MLA_HEAD_PAD), lambda g, i: (g // groups, g % groups)),
                  pl.BlockSpec((seq, heads * MLA_V_DIM), lambda g, i: (g // groups, g % groups))],
        out_specs=pl.BlockSpec((tq, heads * MLA_V_DIM), lambda g, i: ((g // groups) * nq + i, g % groups)),
        compiler_params=_params(("parallel", "arbitrary"), 40),
        name="mla_attn",
    )(q, k, v)


def _mix_kernel(sb_ref, ml_ref, x_ref, mod_ref, gsb_ref, gml_ref, g2_ref, wo_ref, x1_ref, h2_ref):
    a = (_rms(sb_ref[...]) * gsb_ref[...]).astype(BF16)
    b = (_rms(ml_ref[...]) * gml_ref[...]).astype(BF16)
    mixed = _dot(a, wo_ref[:SB_WIDTH, :]) + _dot(b, wo_ref[SB_WIDTH:, :])
    x1 = x_ref[...] + mod_ref[0, 2:3, :] * mixed
    x1_ref[...] = x1
    h2 = _rms(x1) * g2_ref[...]
    h2_ref[...] = (h2 * (1.0 + mod_ref[0, 4:5, :]) + mod_ref[0, 3:4, :]).astype(BF16)


def _mix(sb, ml, x2d, mod3, gsb, gml, g2, wo, seq):
    t, d = x2d.shape
    tm = _tile(seq, 256)
    full = lambda a: pl.BlockSpec(a.shape, lambda i: (0,) * a.ndim)
    return pl.pallas_call(
        _mix_kernel,
        out_shape=(jax.ShapeDtypeStruct((t, d), F32), jax.ShapeDtypeStruct((t, d), BF16)),
        grid=(t // tm,),
        in_specs=[pl.BlockSpec((tm, SB_WIDTH), lambda i: (i, 0)),
                  pl.BlockSpec((tm, MLA_WIDTH), lambda i: (i, 0)),
                  pl.BlockSpec((tm, d), lambda i: (i, 0)),
                  pl.BlockSpec((1, 6, d), lambda i: ((i * tm) // seq, 0, 0)),
                  full(gsb), full(gml), full(g2), full(wo)],
        out_specs=(pl.BlockSpec((tm, d), lambda i: (i, 0)), pl.BlockSpec((tm, d), lambda i: (i, 0))),
        compiler_params=_params(("parallel",), 48),
        name="mix",
    )(sb, ml, x2d, mod3, gsb, gml, g2, wo)


def _top_ranks(s, ids, k):
    rank = jnp.full(s.shape, NOT_SELECTED, F32)
    vals = []
    for r in range(k):
        mx = jnp.max(s, axis=0, keepdims=True)
        first = jnp.min(jnp.where(s == mx, ids, jnp.inf), axis=0, keepdims=True)
        hit = ids == first
        rank = jnp.where(hit, float(r), rank)
        s = jnp.where(hit, -jnp.inf, s)
        vals.append(mx)
    return vals, rank


def _candidate_blocks(k):
    blocks, a = [], 0
    while a < k and k // (a + 1) > 1:
        blocks.append((a, min(k, -(-(k // (a + 1)) // SUBLANES) * SUBLANES)))
        a += 1
    assert (k - a) % SUBLANES == 0
    return blocks, a


def _row_ids(n, m, offset=0, scale=1):
    return ((lax.broadcasted_iota(jnp.int32, (n, m), 0) + offset) * scale).astype(F32)


def _sorting_network(n):
    size = pl.next_power_of_2(n)
    pairs, p = [], 1
    while p < size:
        k = p
        while k >= 1:
            for j in range(k % p, size - k, 2 * k):
                for i in range(min(k, size - j - k)):
                    if (i + j) // (2 * p) == (i + j + k) // (2 * p):
                        pairs.append((i + j, i + j + k))
            k //= 2
        p *= 2
    return [(i, j) for i, j in pairs if j < n]


def _top_distinct(s, k):
    tiles = [s[i:i + SUBLANES] for i in range(0, s.shape[0], SUBLANES)]
    for i, j in _sorting_network(len(tiles)):
        tiles[i], tiles[j] = jnp.maximum(tiles[i], tiles[j]), jnp.minimum(tiles[i], tiles[j])
    vals, distinct = [], None
    for r in range(k + 1):
        mx = jnp.max(tiles[0], axis=0, keepdims=True)
        if vals:
            below = mx < vals[-1]
            distinct = below if distinct is None else jnp.logical_and(distinct, below)
        if r == k:
            break
        vals.append(mx)
        hit = tiles[0] == mx
        depth = min(k - r, len(tiles) - 1)
        for d in range(depth):
            tiles[d] = jnp.where(hit, tiles[d + 1], tiles[d])
        if depth == len(tiles) - 1:
            tiles[depth] = jnp.where(hit, -jnp.inf, tiles[depth])
    return vals, distinct


def _route_column(s1, s2, exact):
    k = PEER_TOPK
    tm = s1.shape[1]
    blocks, tail = _candidate_blocks(k)
    if exact:
        key_ids = _row_ids(PEER_N_KEYS, tm)
        v1, rank1 = _top_ranks(s1, key_ids, k)
        v2, rank2 = _top_ranks(s2, key_ids, k)
    else:
        v1, ok1 = _top_distinct(s1, k)
        v2, ok2 = _top_distinct(s2, k)
        rank2 = jnp.full(s2.shape, NOT_SELECTED, F32)
        for r in reversed(range(k)):
            rank2 = jnp.where(s2 >= v2[r], float(r), rank2)
    top1 = jnp.concatenate(v1, axis=0)
    top2 = jnp.concatenate(v2, axis=0)
    cand = jnp.concatenate([v1[a] + top2[:nb] for a, nb in blocks] + [top1[tail:] + v2[0]], axis=0)
    if exact:
        pos = jnp.concatenate([_row_ids(nb, tm, a * k) for a, nb in blocks]
                              + [_row_ids(k - tail, tm, tail, k)], axis=0)
        best, cand_rank = _top_ranks(cand, pos, k)
        chosen = cand_rank < NOT_SELECTED
        ok = None
    else:
        best, ok3 = _top_distinct(cand, k)
        chosen = cand >= best[-1]
        ok = jnp.logical_and(jnp.logical_and(ok1, ok2), ok3)
    z = jnp.sum(jnp.where(chosen, jnp.exp(cand - best[0]), 0.0), axis=0, keepdims=True)
    picked = chosen.astype(F32)
    cut = jnp.full((PEER_N_KEYS, tm), -1.0, F32)
    row = 0
    for a in range(k):
        nb = blocks[a][1] if a < tail else 1
        count = jnp.sum(picked[row:row + nb], axis=0, keepdims=True) if nb > 1 else picked[row:row + 1]
        is_a = (rank1 == float(a)) if exact else (s1 == v1[a])
        cut = jnp.where(is_a, count - 1.0, cut)
        row += nb
    return cut, jnp.exp(s1 - v1[0]), rank2, jnp.exp(s2 - v2[0]) / z, ok


def _peer_route_kernel(h2_ref, wq_ref, k1_ref, k2_ref, r2_ref, e2_ref, cut_ref, e1_ref, q_scr):
    qp = _dot(h2_ref[...], wq_ref[...]).astype(BF16)
    for h in range(PEER_HEADS):
        q_scr[h] = qp[:, 2 * PEER_HALF * h:2 * PEER_HALF * (h + 1)]
    tm = LANES

    def head(h, carry):
        q = q_scr[h]
        columns = [slice(c * tm, (c + 1) * tm) for c in range(q.shape[0] // tm)]
        scores = [(_dot_nt(k1_ref[h], q[tok, :PEER_HALF]),
                   _dot_nt(k2_ref[h], q[tok, PEER_HALF:])) for tok in columns]

        def store(tok, cut, e1, rank2, e2):
            cut_ref[h, :, tok] = cut
            e1_ref[h, :, tok] = e1
            r2 = rank2.astype(BF16)
            e2 = e2.astype(BF16)
            for g in range(PEER_N_KEYS // BF16_ROWS):
                r2_ref[h, g, :, tok] = r2[g * BF16_ROWS:(g + 1) * BF16_ROWS]
                e2_ref[h, g, :, tok] = e2[g * BF16_ROWS:(g + 1) * BF16_ROWS]

        tie_free = None
        for tok, (s1, s2) in zip(columns, scores):
            *tables, ok = _route_column(s1, s2, exact=False)
            store(tok, *tables)
            tie_free = ok if tie_free is None else jnp.logical_and(tie_free, ok)

        @pl.when(jnp.min(tie_free.astype(F32)) < 0.5)
        def _():
            for tok, (s1, s2) in zip(columns, scores):
                store(tok, *_route_column(s1, s2, exact=True)[:4])
        return carry

    lax.fori_loop(0, PEER_HEADS, head, 0)


def _peer_route(h2, wq, k1, k2):
    t, d = h2.shape
    tm = _tile(t, 512)
    groups = PEER_N_KEYS // BF16_ROWS
    full = lambda a: pl.BlockSpec(a.shape, lambda i: (0,) * a.ndim)
    return pl.pallas_call(
        _peer_route_kernel,
        out_shape=(jax.ShapeDtypeStruct((PEER_HEADS, groups, BF16_ROWS, t), BF16),
                   jax.ShapeDtypeStruct((PEER_HEADS, groups, BF16_ROWS, t), BF16),
                   jax.ShapeDtypeStruct((PEER_HEADS, PEER_N_KEYS, t), F32),
                   jax.ShapeDtypeStruct((PEER_HEADS, PEER_N_KEYS, t), F32)),
        grid=(t // tm,),
        in_specs=[pl.BlockSpec((tm, d), lambda i: (i, 0)), full(wq), full(k1), full(k2)],
        out_specs=(pl.BlockSpec((PEER_HEADS, groups, BF16_ROWS, tm), lambda i: (0, 0, 0, i)),
                   pl.BlockSpec((PEER_HEADS, groups, BF16_ROWS, tm), lambda i: (0, 0, 0, i)),
                   pl.BlockSpec((PEER_HEADS, PEER_N_KEYS, tm), lambda i: (0, 0, i)),
                   pl.BlockSpec((PEER_HEADS, PEER_N_KEYS, tm), lambda i: (0, 0, i))),
        scratch_shapes=[pltpu.VMEM((PEER_HEADS, tm, 2 * PEER_HALF), BF16)],
        compiler_params=_params(("parallel",), 48),
        name="peer_route",
    )(h2, wq, k1, k2)


def _peer_dense_kernel(h2_ref, u_ref, vt_ref, r2_ref, e2_ref, cut_ref, e1_ref, o_ref, pre_scr, gate_scr, *, rows):
    j = pl.program_id(1)
    d, tm = o_ref.shape
    te = rows * PEER_N_KEYS
    groups = PEER_N_KEYS // BF16_ROWS
    n_chunks, n_k, n_kt = tm // MXU_WIDTH, d // MXU_WIDTH, te // MXU_WIDTH

    @pl.when(j == 0)
    def _():
        o_ref[...] = jnp.zeros_like(o_ref)

    zero = jnp.zeros((BF16_ROWS, tm), BF16)
    row_tables = {}

    def gate_piece(r, g):
        if r not in row_tables:
            i1 = j * rows + r
            row_tables[r] = [(jnp.broadcast_to(cut_ref[h, pl.ds(i1, 1), :], (BF16_ROWS, tm)).astype(BF16),
                              jnp.broadcast_to(e1_ref[h, pl.ds(i1, 1), :], (BF16_ROWS, tm)).astype(BF16))
                             for h in range(PEER_HEADS)]
        gate = None
        for h, (cut, e1) in enumerate(row_tables[r]):
            term = jnp.where(r2_ref[h, g] <= cut, e2_ref[h, g], zero) * e1
            gate = term if gate is None else gate + term
        first = r * PEER_N_KEYS + g * BF16_ROWS
        gate_scr[first:first + BF16_ROWS, :] = gate

    partial_sums = {}

    def pre_piece(c, k):
        cols = slice(c * MXU_WIDTH, (c + 1) * MXU_WIDTH)
        ks = slice(k * MXU_WIDTH, (k + 1) * MXU_WIDTH)
        part = _dot_nt(u_ref[:, ks], h2_ref[cols, ks])
        partial_sums[c] = part if k == 0 else partial_sums[c] + part
        if k == n_k - 1:
            pre_scr[:, cols] = partial_sums.pop(c)

    _emit_alternately([functools.partial(pre_piece, c, k) for c in range(n_chunks) for k in range(n_k)],
                      [functools.partial(gate_piece, r, g) for r in range(rows) for g in range(groups)])

    for c in range(n_chunks):
        cols = slice(c * MXU_WIDTH, (c + 1) * MXU_WIDTH)
        out = None
        for kt in range(n_kt):
            rs = slice(kt * MXU_WIDTH, (kt + 1) * MXU_WIDTH)
            pre = pre_scr[rs, cols]
            act = (0.5 * pre * (1.0 + lax.erf(pre * (2.0 ** -0.5)))).astype(BF16)
            part = _dot(vt_ref[:, rs], act * gate_scr[rs, cols])
            out = part if out is None else out + part
        o_ref[:, cols] += out


def _peer_dense(h2, u, vt, r2, e2, cut, e1):
    t, d = h2.shape
    n_exp = u.shape[0]
    tm = _tile(t, 1024)
    te = 512
    groups = PEER_N_KEYS // BF16_ROWS
    once = pl.Buffered(1)
    return pl.pallas_call(
        functools.partial(_peer_dense_kernel, rows=te // PEER_N_KEYS),
        out_shape=jax.ShapeDtypeStruct((d, t), F32),
        grid=(t // tm, n_exp // te),
        in_specs=[pl.BlockSpec((tm, d), lambda i, j: (i, 0), pipeline_mode=once),
                  pl.BlockSpec((te, d), lambda i, j: (j, 0)),
                  pl.BlockSpec((d, te), lambda i, j: (0, j)),
                  pl.BlockSpec((PEER_HEADS, groups, BF16_ROWS, tm), lambda i, j: (0, 0, 0, i), pipeline_mode=once),
                  pl.BlockSpec((PEER_HEADS, groups, BF16_ROWS, tm), lambda i, j: (0, 0, 0, i), pipeline_mode=once),
                  pl.BlockSpec((PEER_HEADS, PEER_N_KEYS, tm), lambda i, j: (0, 0, i), pipeline_mode=once),
                  pl.BlockSpec((PEER_HEADS, PEER_N_KEYS, tm), lambda i, j: (0, 0, i), pipeline_mode=once)],
        out_specs=pl.BlockSpec((d, tm), lambda i, j: (0, i)),
        scratch_shapes=[pltpu.VMEM((te, tm), F32), pltpu.VMEM((te, tm), BF16)],
        compiler_params=_params(("parallel", "arbitrary"), 56),
        name="peer_dense",
    )(h2, u, vt, r2, e2, cut, e1)


def _residual_kernel(x1_ref, pt_ref, mod_ref, gf_ref, o_ref, *, final_norm):
    x2 = x1_ref[...] + mod_ref[0, 5:6, :] * pt_ref[...].T
    o_ref[...] = _rms(x2) * gf_ref[...] if final_norm else x2


def _residual(x1, peer_t, mod3, gf, seq, final_norm):
    t, d = x1.shape
    tm = _tile(seq, 512)
    return pl.pallas_call(
        functools.partial(_residual_kernel, final_norm=final_norm),
        out_shape=jax.ShapeDtypeStruct((t, d), F32),
        grid=(t // tm,),
        in_specs=[pl.BlockSpec((tm, d), lambda i: (i, 0)),
                  pl.BlockSpec((d, tm), lambda i: (0, i)),
                  pl.BlockSpec((1, 6, d), lambda i: ((i * tm) // seq, 0, 0)),
                  pl.BlockSpec((1, d), lambda i: (0, 0))],
        out_specs=pl.BlockSpec((tm, d), lambda i: (i, 0)),
        compiler_params=_params(("parallel",), 40),
        name="residual",
    )(x1, peer_t, mod3, gf)


def _rot_half_cols(w):
    half = w.shape[-1] // 2
    return jnp.concatenate([-w[..., half:], w[..., :half]], axis=-1)


def _pad_cols(w, width):
    return jnp.concatenate([w, jnp.zeros(w.shape[:-1] + (width - w.shape[-1],), w.dtype)], axis=-1)


def _layer_weights(w_in, w_uq, w_ukv):
    n_heads_cols = 3 * SB_WIDTH
    w_kr = w_in[:, n_heads_cols + MLA_Q_RANK + MLA_KV_RANK:]
    w_latent = jnp.concatenate([w_in[:, n_heads_cols:n_heads_cols + MLA_Q_RANK + MLA_KV_RANK],
                                _pad_cols(w_kr, LANES), _pad_cols(_rot_half_cols(w_kr), LANES)], axis=1)
    wq = w_uq.reshape(MLA_Q_RANK, MLA_HEADS, MLA_QK_DIM)
    wq_pe = wq[..., MLA_NOPE_DIM:]
    wq_ext = jnp.concatenate([wq[..., :MLA_NOPE_DIM], _pad_cols(wq_pe, LANES),
                              _pad_cols(_rot_half_cols(wq_pe), LANES)], axis=-1)
    wq_ext = wq_ext.reshape(MLA_Q_RANK, MLA_HEADS * 3 * LANES)
    wkv = w_ukv.reshape(MLA_KV_RANK, MLA_HEADS, MLA_NOPE_DIM + MLA_V_DIM)
    wkn = wkv[..., :MLA_NOPE_DIM].reshape(MLA_KV_RANK, MLA_HEADS * MLA_NOPE_DIM)
    wv = wkv[..., MLA_NOPE_DIM:].reshape(MLA_KV_RANK, MLA_WIDTH)
    return w_in.astype(BF16), w_latent.astype(BF16), wq_ext.astype(BF16), wkn.astype(BF16), wv.astype(BF16)


def kernel(x, c, positions, w_ada, b_ada, norm1_g, w_in, mla_q_norm_g, w_uq, mla_kv_norm_g, w_ukv,
           out_norm_sba_g, out_norm_mla_g, w_o, norm2_g, w_peer_q, peer_keys1, peer_keys2,
           peer_u, peer_v, final_norm_g):
    batch, seq, d = x.shape
    depth = w_ada.shape[0]
    t = batch * seq
    x2d = x.reshape(t, d)
    pos = positions.reshape(t, 1).astype(F32)
    inv_freq = ROPE_THETA ** (-jnp.arange(0, MLA_ROPE_DIM, 2, dtype=F32) / MLA_ROPE_DIM)
    invf = _pad_cols(jnp.concatenate([inv_freq, inv_freq])[None, :], LANES)
    c_rows = 16
    c_pad = jnp.concatenate([c, jnp.zeros((c_rows - batch, d), c.dtype)], axis=0)
    row = lambda g: g[None, :]

    for l in range(depth):
        mod = _adaln(c_pad, w_ada[l], b_ada[l][None, :])
        mod3 = mod[:batch].reshape(batch, 6, d)
        w_heads, w_latent, wq_ext, wkn, wv = _layer_weights(w_in[l], w_uq[l], w_ukv[l])

        qkv, latent = _inproj(x2d, mod3, row(norm1_g[l]), w_heads, w_latent, seq)
        q_mla, k_mla, v_mla = _mla_proj(latent, pos, invf, row(mla_q_norm_g[l]), row(mla_kv_norm_g[l]),
                                        wq_ext, wkn, wv)
        sb_out = _sb_attn(qkv, batch, seq)
        mla_out = _mla_attn(q_mla, k_mla, v_mla, batch, seq)
        x1, h2 = _mix(sb_out, mla_out, x2d, mod3, row(out_norm_sba_g[l]), row(out_norm_mla_g[l]),
                      row(norm2_g[l]), w_o[l].astype(BF16), seq)
        routing = _peer_route(h2, w_peer_q[l].astype(BF16), peer_keys1[l].astype(BF16),
                              peer_keys2[l].astype(BF16))
        peer = _peer_dense(h2, peer_u[l].astype(BF16), peer_v[l].T.astype(BF16), *routing)
        x2d = _residual(x1, peer, mod3, row(final_norm_g), seq, final_norm=(l == depth - 1))
    return x2d.reshape(batch, seq, d)
```

```python
import functools

import jax
import jax.numpy as jnp
from jax import lax
from jax.experimental import pallas as pl
from jax.experimental.pallas import tpu as pltpu

F32 = jnp.float32
BF16 = jnp.bfloat16

SB_HEADS = 8
SB_HEAD_DIM = 128
SB_WIDTH = SB_HEADS * SB_HEAD_DIM
MLA_HEADS = 8
MLA_NOPE_DIM = 128
MLA_ROPE_DIM = 64
MLA_V_DIM = 128
MLA_Q_RANK = 512
MLA_KV_RANK = 256
MLA_QK_DIM = MLA_NOPE_DIM + MLA_ROPE_DIM
MLA_WIDTH = MLA_HEADS * MLA_V_DIM
ROPE_THETA = 10000.0
PEER_HEADS = 8
PEER_N_KEYS = 128
PEER_HALF = 128
PEER_TOPK = 16
NORM_EPS = 1e-6
SB_SCALE = SB_HEAD_DIM ** -0.5
MLA_SCALE = MLA_QK_DIM ** -0.5
SB_UNDERFLOW = -105.0

LANES = 128
MXU_WIDTH = 256
V7X_VMEM_BYTES = 64 * 1024 * 1024
MLA_HEAD_PAD = 2 * LANES
LATENT_WIDTH = MLA_Q_RANK + MLA_KV_RANK + 2 * LANES
SUBLANES = 8
BF16_ROWS = 16
NOT_SELECTED = 127.0


def _tile(n, pref):
    if n <= pref:
        return n
    t = pref
    while n % t:
        t -= 1
    return t


def _params(semantics, vmem_mib):
    return pltpu.CompilerParams(dimension_semantics=semantics,
                                vmem_limit_bytes=min(vmem_mib * 1024 * 1024, V7X_VMEM_BYTES - 4 * 1024 * 1024))


def _rms(x):
    return x * lax.rsqrt(jnp.mean(x * x, axis=-1, keepdims=True) + NORM_EPS)


def _dot(a, b):
    return jnp.dot(a, b, preferred_element_type=F32)


def _emit_alternately(first, second):
    n, m = len(first), len(second)
    i = k = 0
    while i < n or k < m:
        if k >= m or (i < n and i * m <= k * n):
            first[i]()
            i += 1
        else:
            second[k]()
            k += 1


def _dot_nt(a, b):
    return lax.dot_general(a, b, (((1,), (1,)), ((), ())), preferred_element_type=F32)


def _adaln_kernel(c_ref, w_ref, b_ref, o_ref):
    c = c_ref[...]
    c_act = c / (1.0 + jnp.exp(-c))
    o_ref[...] = _dot(c_act.astype(BF16), w_ref[...].astype(BF16)) + b_ref[...]


def _adaln(c_pad, w, b):
    rows, d = c_pad.shape
    n = w.shape[1]
    tn = _tile(n, 1024)
    return pl.pallas_call(
        _adaln_kernel,
        out_shape=jax.ShapeDtypeStruct((rows, n), F32),
        grid=(n // tn,),
        in_specs=[pl.BlockSpec((rows, d), lambda j: (0, 0)),
                  pl.BlockSpec((d, tn), lambda j: (0, j)),
                  pl.BlockSpec((1, tn), lambda j: (0, j))],
        out_specs=pl.BlockSpec((rows, tn), lambda j: (0, j)),
        compiler_params=_params(("arbitrary",), 40),
        name="adaln",
    )(c_pad, w, b)


def _inproj_kernel(x_ref, mod_ref, g_ref, wh_ref, wl_ref, oa_ref, ob_ref, h_scr, *, n_head_tiles):
    j = pl.program_id(1)

    @pl.when(j == 0)
    def _():
        y = _rms(x_ref[...]) * g_ref[...]
        h = (y * (1.0 + mod_ref[0, 1:2, :]) + mod_ref[0, 0:1, :]).astype(BF16)
        h_scr[...] = h
        oa_ref[...] = (_dot(h, wh_ref[...]) * SB_SCALE).astype(BF16)

    @pl.when(jnp.logical_and(j > 0, j < n_head_tiles))
    def _():
        oa_ref[...] = _dot(h_scr[...], wh_ref[...]).astype(BF16)

    @pl.when(j == n_head_tiles)
    def _():
        ob_ref[...] = _dot(h_scr[...], wl_ref[...])


def _inproj(x2d, mod3, g, w_heads, w_latent, seq):
    t, d = x2d.shape
    tn = SB_WIDTH
    n_head_tiles = 3
    assert w_latent.shape[1] == LATENT_WIDTH
    tm = _tile(seq, 512)
    return pl.pallas_call(
        functools.partial(_inproj_kernel, n_head_tiles=n_head_tiles),
        out_shape=(jax.ShapeDtypeStruct((t, n_head_tiles * tn), BF16),
                   jax.ShapeDtypeStruct((t, LATENT_WIDTH), F32)),
        grid=(t // tm, n_head_tiles + 1),
        in_specs=[pl.BlockSpec((tm, d), lambda i, j: (i, 0)),
                  pl.BlockSpec((1, 6, d), lambda i, j: ((i * tm) // seq, 0, 0)),
                  pl.BlockSpec((1, d), lambda i, j: (0, 0)),
                  pl.BlockSpec((d, tn), lambda i, j: (0, jnp.minimum(j, n_head_tiles - 1))),
                  pl.BlockSpec((d, LATENT_WIDTH), lambda i, j: (0, 0))],
        out_specs=(pl.BlockSpec((tm, tn), lambda i, j: (i, jnp.minimum(j, n_head_tiles - 1))),
                   pl.BlockSpec((tm, LATENT_WIDTH), lambda i, j: (i, 0))),
        scratch_shapes=[pltpu.VMEM((tm, d), BF16)],
        compiler_params=_params(("parallel", "arbitrary"), 48),
        name="inproj",
    )(x2d, mod3, g, w_heads, w_latent)


def _mla_proj_kernel(c_ref, pos_ref, invf_ref, gq_ref, gkv_ref, wq_ref, wkn_ref, wv_ref,
                     q_ref, k_ref, v_ref):
    cqn = (_rms(c_ref[:, :MLA_Q_RANK]) * gq_ref[...]).astype(BF16)
    ckvn = (_rms(c_ref[:, MLA_Q_RANK:MLA_Q_RANK + MLA_KV_RANK]) * gkv_ref[...]).astype(BF16)
    kr = c_ref[:, MLA_Q_RANK + MLA_KV_RANK:MLA_Q_RANK + MLA_KV_RANK + LANES]
    kr_rot = c_ref[:, MLA_Q_RANK + MLA_KV_RANK + LANES:]
    ang = pos_ref[...] * invf_ref[...]
    cos = jnp.cos(ang)
    sin = jnp.sin(ang)
    k_pe = (kr * cos + kr_rot * sin).astype(BF16)
    qa = _dot(cqn, wq_ref[...])
    kn = _dot(ckvn, wkn_ref[...])
    v_ref[...] = _dot(ckvn, wv_ref[...]).astype(BF16)
    for h in range(MLA_HEADS):
        b = 3 * LANES * h
        o = MLA_HEAD_PAD * h
        q_ref[:, o:o + LANES] = (qa[:, b:b + LANES] * MLA_SCALE).astype(BF16)
        q_pe = qa[:, b + LANES:b + 2 * LANES] * cos + qa[:, b + 2 * LANES:b + 3 * LANES] * sin
        q_ref[:, o + LANES:o + 2 * LANES] = (q_pe * MLA_SCALE).astype(BF16)
        k_ref[:, o:o + LANES] = kn[:, h * LANES:(h + 1) * LANES].astype(BF16)
        k_ref[:, o + LANES:o + 2 * LANES] = k_pe


def _mla_proj(latent, pos, invf, gq, gkv, wq, wkn, wv):
    t = latent.shape[0]
    tm = _tile(t, 512)
    full = lambda a: pl.BlockSpec(a.shape, lambda i: (0,) * a.ndim)
    return pl.pallas_call(
        _mla_proj_kernel,
        out_shape=(jax.ShapeDtypeStruct((t, MLA_HEADS * MLA_HEAD_PAD), BF16),
                   jax.ShapeDtypeStruct((t, MLA_HEADS * MLA_HEAD_PAD), BF16),
                   jax.ShapeDtypeStruct((t, MLA_WIDTH), BF16)),
        grid=(t // tm,),
        in_specs=[pl.BlockSpec((tm, LATENT_WIDTH), lambda i: (i, 0)),
                  pl.BlockSpec((tm, 1), lambda i: (i, 0)),
                  full(invf), full(gq), full(gkv), full(wq), full(wkn), full(wv)],
        out_specs=(pl.BlockSpec((tm, MLA_HEADS * MLA_HEAD_PAD), lambda i: (i, 0)),
                   pl.BlockSpec((tm, MLA_HEADS * MLA_HEAD_PAD), lambda i: (i, 0)),
                   pl.BlockSpec((tm, MLA_WIDTH), lambda i: (i, 0))),
        compiler_params=_params(("parallel",), 48),
        name="mla_proj",
    )(latent, pos, invf, gq, gkv, wq, wkn, wv)


def _sb_attn_kernel(q_ref, k_ref, v_ref, o_ref, *, tq, tk, heads):
    qi = pl.program_id(1)
    n_diag = tq // tk
    key_j = lax.broadcasted_iota(jnp.int32, (tk, tk), 0)
    key_s = lax.broadcasted_iota(jnp.int32, (tk, tk), 1)
    later = (key_j > key_s).astype(BF16)
    row = lax.broadcasted_iota(jnp.int32, (tq, tk), 0)
    col = lax.broadcasted_iota(jnp.int32, (tq, tk), 1)

    def head_block(h, kb, acc, run, diag):
        lanes = slice(h * SB_HEAD_DIM, (h + 1) * SB_HEAD_DIM)
        start = pl.multiple_of(kb * tk, tk)
        z = _dot_nt(q_ref[:, lanes], k_ref[pl.ds(start, tk), lanes])
        t = jnp.log(1.0 + jnp.exp(-jnp.abs(z)))
        log_1m = -(jnp.maximum(z, 0.0) + t)
        log_b = jnp.minimum(z, 0.0) - t
        if diag is not None:
            causal = col + diag * tk < row
            log_1m = jnp.where(causal, log_1m, 0.0)
        hi = log_1m.astype(BF16)
        lo = (log_1m - hi.astype(F32)).astype(BF16)
        sums = _dot(jnp.concatenate([hi, lo], axis=0), later)
        between = sums[:tq] + sums[tq:]
        w = jnp.exp(log_b + between + run)
        if diag is not None:
            w = jnp.where(causal, w, 0.0)
        acc = acc + _dot(w.astype(BF16), v_ref[pl.ds(start, tk), lanes])
        run = run + jnp.sum(log_1m, axis=1, keepdims=True)
        return acc, run

    def block(kb, carry, diag=None):
        return tuple(head_block(h, kb, *carry[h], diag) for h in range(heads))

    carry = tuple((jnp.zeros((tq, SB_HEAD_DIM), F32), jnp.zeros((tq, 1), F32)) for _ in range(heads))
    for d in reversed(range(n_diag)):
        carry = block(qi * n_diag + d, carry, d)
    n_below = qi * n_diag

    def live(state):
        i, carry = state
        slowest = functools.reduce(jnp.maximum, [jnp.max(c[1]) for c in carry])
        return jnp.logical_and(i < n_below, slowest > SB_UNDERFLOW)

    def step(state):
        i, carry = state
        return i + 1, block(n_below - 1 - i, carry)

    _, carry = lax.while_loop(live, step, (jnp.int32(0), carry))
    for h in range(heads):
        o_ref[:, h * SB_HEAD_DIM:(h + 1) * SB_HEAD_DIM] = carry[h][0]


def _sb_attn(qkv, batch, seq):
    t = qkv.shape[0]
    tq = _tile(seq, 512)
    tk = _tile(tq, 256)
    nq = seq // tq
    heads = 2
    groups = SB_HEADS // heads
    width = heads * SB_HEAD_DIM
    return pl.pallas_call(
        functools.partial(_sb_attn_kernel, tq=tq, tk=tk, heads=heads),
        out_shape=jax.ShapeDtypeStruct((t, SB_WIDTH), F32),
        grid=(batch * groups, nq),
        in_specs=[pl.BlockSpec((tq, width), lambda g, i: ((g // groups) * nq + i, g % groups)),
                  pl.BlockSpec((seq, width), lambda g, i: (g // groups, groups + g % groups)),
                  pl.BlockSpec((seq, width), lambda g, i: (g // groups, 2 * groups + g % groups))],
        out_specs=pl.BlockSpec((tq, width), lambda g, i: ((g // groups) * nq + i, g % groups)),
        compiler_params=_params(("parallel", "arbitrary"), 32),
        name="sb_attn",
    )(qkv, qkv, qkv)


def _mla_attn_kernel(q_ref, k_ref, v_ref, o_ref, *, tq, tk, heads):
    qi = pl.program_id(1)
    n_diag = tq // tk
    row = lax.broadcasted_iota(jnp.int32, (tq, tk), 0)
    col = lax.broadcasted_iota(jnp.int32, (tq, tk), 1)

    def head_block(h, kb, acc, m, l, diag):
        start = pl.multiple_of(kb * tk, tk)
        s = _dot_nt(q_ref[:, h * MLA_HEAD_PAD:(h + 1) * MLA_HEAD_PAD],
                    k_ref[pl.ds(start, tk), h * MLA_HEAD_PAD:(h + 1) * MLA_HEAD_PAD])
        if diag is not None:
            s = jnp.where(col + diag * tk <= row, s, -jnp.inf)
        m_new = jnp.maximum(m, jnp.max(s, axis=1, keepdims=True))
        alpha = jnp.exp(m - m_new)
        p = jnp.exp(s - m_new)
        l = alpha * l + jnp.sum(p, axis=1, keepdims=True)
        acc = alpha * acc + _dot(p.astype(BF16), v_ref[pl.ds(start, tk), h * MLA_V_DIM:(h + 1) * MLA_V_DIM])
        return acc, m_new, l

    def block(kb, carry, diag=None):
        return tuple(head_block(h, kb, *carry[h], diag) for h in range(heads))

    carry = tuple((jnp.zeros((tq, MLA_V_DIM), F32), jnp.full((tq, 1), -jnp.inf, F32), jnp.zeros((tq, 1), F32))
                  for _ in range(heads))
    for d in range(n_diag):
        carry = block(qi * n_diag + d, carry, d)
    carry = lax.fori_loop(0, qi * n_diag, lambda i, c: block(qi * n_diag - 1 - i, c), carry)
    for h in range(heads):
        acc, _, l = carry[h]
        o_ref[:, h * MLA_V_DIM:(h + 1) * MLA_V_DIM] = acc / l


def _mla_attn(q, k, v, batch, seq):
    t = q.shape[0]
    tq = _tile(seq, 512)
    tk = tq
    nq = seq // tq
    heads = 2
    groups = MLA_HEADS // heads
    return pl.pallas_call(
        functools.partial(_mla_attn_kernel, tq=tq, tk=tk, heads=heads),
        out_shape=jax.ShapeDtypeStruct((t, MLA_WIDTH), F32),
        grid=(batch * groups, nq),
        in_specs=[pl.BlockSpec((tq, heads * MLA_HEAD_PAD), lambda g, i: ((g // groups) * nq + i, g % groups)),
                  pl.BlockSpec((seq, heads * MLA_HEAD_PAD), lambda g, i: (g // groups, g % groups)),
                  pl.BlockSpec((seq, heads * MLA_V_DIM), lambda g, i: (g // groups, g % groups))],
        out_specs=pl.BlockSpec((tq, heads * MLA_V_DIM), lambda g, i: ((g // groups) * nq + i, g % groups)),
        compiler_params=_params(("parallel", "arbitrary"), 40),
        name="mla_attn",
    )(q, k, v)


def _mix_kernel(sb_ref, ml_ref, x_ref, mod_ref, gsb_ref, gml_ref, g2_ref, wo_ref, x1_ref, h2_ref):
    a = (_rms(sb_ref[...]) * gsb_ref[...]).astype(BF16)
    b = (_rms(ml_ref[...]) * gml_ref[...]).astype(BF16)
    mixed = _dot(a, wo_ref[:SB_WIDTH, :]) + _dot(b, wo_ref[SB_WIDTH:, :])
    x1 = x_ref[...] + mod_ref[0, 2:3, :] * mixed
    x1_ref[...] = x1
    h2 = _rms(x1) * g2_ref[...]
    h2_ref[...] = (h2 * (1.0 + mod_ref[0, 4:5, :]) + mod_ref[0, 3:4, :]).astype(BF16)


def _mix(sb, ml, x2d, mod3, gsb, gml, g2, wo, seq):
    t, d = x2d.shape
    tm = _tile(seq, 256)
    full = lambda a: pl.BlockSpec(a.shape, lambda i: (0,) * a.ndim)
    return pl.pallas_call(
        _mix_kernel,
        out_shape=(jax.ShapeDtypeStruct((t, d), F32), jax.ShapeDtypeStruct((t, d), BF16)),
        grid=(t // tm,),
        in_specs=[pl.BlockSpec((tm, SB_WIDTH), lambda i: (i, 0)),
                  pl.BlockSpec((tm, MLA_WIDTH), lambda i: (i, 0)),
                  pl.BlockSpec((tm, d), lambda i: (i, 0)),
                  pl.BlockSpec((1, 6, d), lambda i: ((i * tm) // seq, 0, 0)),
                  full(gsb), full(gml), full(g2), full(wo)],
        out_specs=(pl.BlockSpec((tm, d), lambda i: (i, 0)), pl.BlockSpec((tm, d), lambda i: (i, 0))),
        compiler_params=_params(("parallel",), 48),
        name="mix",
    )(sb, ml, x2d, mod3, gsb, gml, g2, wo)


def _top_ranks(s, ids, k):
    rank = jnp.full(s.shape, NOT_SELECTED, F32)
    vals = []
    for r in range(k):
        mx = jnp.max(s, axis=0, keepdims=True)
        first = jnp.min(jnp.where(s == mx, ids, jnp.inf), axis=0, keepdims=True)
        hit = ids == first
        rank = jnp.where(hit, float(r), rank)
        s = jnp.where(hit, -jnp.inf, s)
        vals.append(mx)
    return vals, rank


def _candidate_blocks(k):
    blocks, a = [], 0
    while a < k and k // (a + 1) > 1:
        blocks.append((a, min(k, -(-(k // (a + 1)) // SUBLANES) * SUBLANES)))
        a += 1
    assert (k - a) % SUBLANES == 0
    return blocks, a


def _row_ids(n, m, offset=0, scale=1):
    return ((lax.broadcasted_iota(jnp.int32, (n, m), 0) + offset) * scale).astype(F32)


def _sorting_network(n):
    size = pl.next_power_of_2(n)
    pairs, p = [], 1
    while p < size:
        k = p
        while k >= 1:
            for j in range(k % p, size - k, 2 * k):
                for i in range(min(k, size - j - k)):
                    if (i + j) // (2 * p) == (i + j + k) // (2 * p):
                        pairs.append((i + j, i + j + k))
            k //= 2
        p *= 2
    return [(i, j) for i, j in pairs if j < n]


def _top_distinct(s, k):
    tiles = [s[i:i + SUBLANES] for i in range(0, s.shape[0], SUBLANES)]
    for i, j in _sorting_network(len(tiles)):
        tiles[i], tiles[j] = jnp.maximum(tiles[i], tiles[j]), jnp.minimum(tiles[i], tiles[j])
    vals, distinct = [], None
    for r in range(k + 1):
        mx = jnp.max(tiles[0], axis=0, keepdims=True)
        if vals:
            below = mx < vals[-1]
            distinct = below if distinct is None else jnp.logical_and(distinct, below)
        if r == k:
            break
        vals.append(mx)
        hit = tiles[0] == mx
        depth = min(k - r, len(tiles) - 1)
        for d in range(depth):
            tiles[d] = jnp.where(hit, tiles[d + 1], tiles[d])
        if depth == len(tiles) - 1:
            tiles[depth] = jnp.where(hit, -jnp.inf, tiles[depth])
    return vals, distinct


def _route_column(s1, s2, exact):
    k = PEER_TOPK
    tm = s1.shape[1]
    blocks, tail = _candidate_blocks(k)
    if exact:
        key_ids = _row_ids(PEER_N_KEYS, tm)
        v1, rank1 = _top_ranks(s1, key_ids, k)
        v2, rank2 = _top_ranks(s2, key_ids, k)
    else:
        v1, ok1 = _top_distinct(s1, k)
        v2, ok2 = _top_distinct(s2, k)
        rank2 = jnp.full(s2.shape, NOT_SELECTED, F32)
        for r in reversed(range(k)):
            rank2 = jnp.where(s2 >= v2[r], float(r), rank2)
    top1 = jnp.concatenate(v1, axis=0)
    top2 = jnp.concatenate(v2, axis=0)
    cand = jnp.concatenate([v1[a] + top2[:nb] for a, nb in blocks] + [top1[tail:] + v2[0]], axis=0)
    if exact:
        pos = jnp.concatenate([_row_ids(nb, tm, a * k) for a, nb in blocks]
                              + [_row_ids(k - tail, tm, tail, k)], axis=0)
        best, cand_rank = _top_ranks(cand, pos, k)
        chosen = cand_rank < NOT_SELECTED
        ok = None
    else:
        best, ok3 = _top_distinct(cand, k)
        chosen = cand >= best[-1]
        ok = jnp.logical_and(jnp.logical_and(ok1, ok2), ok3)
    z = jnp.sum(jnp.where(chosen, jnp.exp(cand - best[0]), 0.0), axis=0, keepdims=True)
    picked = chosen.astype(F32)
    cut = jnp.full((PEER_N_KEYS, tm), -1.0, F32)
    row = 0
    for a in range(k):
        nb = blocks[a][1] if a < tail else 1
        count = jnp.sum(picked[row:row + nb], axis=0, keepdims=True) if nb > 1 else picked[row:row + 1]
        is_a = (rank1 == float(a)) if exact else (s1 == v1[a])
        cut = jnp.where(is_a, count - 1.0, cut)
        row += nb
    return cut, jnp.exp(s1 - v1[0]), rank2, jnp.exp(s2 - v2[0]) / z, ok


def _peer_route_kernel(h2_ref, wq_ref, k1_ref, k2_ref, r2_ref, e2_ref, cut_ref, e1_ref, q_scr):
    qp = _dot(h2_ref[...], wq_ref[...]).astype(BF16)
    for h in range(PEER_HEADS):
        q_scr[h] = qp[:, 2 * PEER_HALF * h:2 * PEER_HALF * (h + 1)]
    tm = LANES

    def head(h, carry):
        q = q_scr[h]
        columns = [slice(c * tm, (c + 1) * tm) for c in range(q.shape[0] // tm)]
        scores = [(_dot_nt(k1_ref[h], q[tok, :PEER_HALF]),
                   _dot_nt(k2_ref[h], q[tok, PEER_HALF:])) for tok in columns]

        def store(tok, cut, e1, rank2, e2):
            cut_ref[h, :, tok] = cut
            e1_ref[h, :, tok] = e1
            r2 = rank2.astype(BF16)
            e2 = e2.astype(BF16)
            for g in range(PEER_N_KEYS // BF16_ROWS):
                r2_ref[h, g, :, tok] = r2[g * BF16_ROWS:(g + 1) * BF16_ROWS]
                e2_ref[h, g, :, tok] = e2[g * BF16_ROWS:(g + 1) * BF16_ROWS]

        tie_free = None
        for tok, (s1, s2) in zip(columns, scores):
            *tables, ok = _route_column(s1, s2, exact=False)
            store(tok, *tables)
            tie_free = ok if tie_free is None else jnp.logical_and(tie_free, ok)

        @pl.when(jnp.min(tie_free.astype(F32)) < 0.5)
        def _():
            for tok, (s1, s2) in zip(columns, scores):
                store(tok, *_route_column(s1, s2, exact=True)[:4])
        return carry

    lax.fori_loop(0, PEER_HEADS, head, 0)


def _peer_route(h2, wq, k1, k2):
    t, d = h2.shape
    tm = _tile(t, 512)
    groups = PEER_N_KEYS // BF16_ROWS
    full = lambda a: pl.BlockSpec(a.shape, lambda i: (0,) * a.ndim)
    return pl.pallas_call(
        _peer_route_kernel,
        out_shape=(jax.ShapeDtypeStruct((PEER_HEADS, groups, BF16_ROWS, t), BF16),
                   jax.ShapeDtypeStruct((PEER_HEADS, groups, BF16_ROWS, t), BF16),
                   jax.ShapeDtypeStruct((PEER_HEADS, PEER_N_KEYS, t), F32),
                   jax.ShapeDtypeStruct((PEER_HEADS, PEER_N_KEYS, t), F32)),
        grid=(t // tm,),
        in_specs=[pl.BlockSpec((tm, d), lambda i: (i, 0)), full(wq), full(k1), full(k2)],
        out_specs=(pl.BlockSpec((PEER_HEADS, groups, BF16_ROWS, tm), lambda i: (0, 0, 0, i)),
                   pl.BlockSpec((PEER_HEADS, groups, BF16_ROWS, tm), lambda i: (0, 0, 0, i)),
                   pl.BlockSpec((PEER_HEADS, PEER_N_KEYS, tm), lambda i: (0, 0, i)),
                   pl.BlockSpec((PEER_HEADS, PEER_N_KEYS, tm), lambda i: (0, 0, i))),
        scratch_shapes=[pltpu.VMEM((PEER_HEADS, tm, 2 * PEER_HALF), BF16)],
        compiler_params=_params(("parallel",), 48),
        name="peer_route",
    )(h2, wq, k1, k2)


def _peer_dense_kernel(h2_ref, u_ref, vt_ref, r2_ref, e2_ref, cut_ref, e1_ref, o_ref, pre_scr, gate_scr, *, rows):
    j = pl.program_id(1)
    d, tm = o_ref.shape
    te = rows * PEER_N_KEYS
    groups = PEER_N_KEYS // BF16_ROWS
    n_chunks, n_k, n_kt = tm // MXU_WIDTH, d // MXU_WIDTH, te // MXU_WIDTH

    @pl.when(j == 0)
    def _():
        o_ref[...] = jnp.zeros_like(o_ref)

    zero = jnp.zeros((BF16_ROWS, tm), BF16)
    row_tables = {}

    def gate_piece(r, g):
        if r not in row_tables:
            i1 = j * rows + r
            row_tables[r] = [(jnp.broadcast_to(cut_ref[h, pl.ds(i1, 1), :], (BF16_ROWS, tm)).astype(BF16),
                              jnp.broadcast_to(e1_ref[h, pl.ds(i1, 1), :], (BF16_ROWS, tm)).astype(BF16))
                             for h in range(PEER_HEADS)]
        gate = None
        for h, (cut, e1) in enumerate(row_tables[r]):
            term = jnp.where(r2_ref[h, g] <= cut, e2_ref[h, g], zero) * e1
            gate = term if gate is None else gate + term
        first = r * PEER_N_KEYS + g * BF16_ROWS
        gate_scr[first:first + BF16_ROWS, :] = gate

    partial_sums = {}

    def pre_piece(c, k):
        cols = slice(c * MXU_WIDTH, (c + 1) * MXU_WIDTH)
        ks = slice(k * MXU_WIDTH, (k + 1) * MXU_WIDTH)
        part = _dot_nt(u_ref[:, ks], h2_ref[cols, ks])
        partial_sums[c] = part if k == 0 else partial_sums[c] + part
        if k == n_k - 1:
            pre_scr[:, cols] = partial_sums.pop(c)

    _emit_alternately([functools.partial(pre_piece, c, k) for c in range(n_chunks) for k in range(n_k)],
                      [functools.partial(gate_piece, r, g) for r in range(rows) for g in range(groups)])

    for c in range(n_chunks):
        cols = slice(c * MXU_WIDTH, (c + 1) * MXU_WIDTH)
        out = None
        for kt in range(n_kt):
            rs = slice(kt * MXU_WIDTH, (kt + 1) * MXU_WIDTH)
            pre = pre_scr[rs, cols]
            act = (0.5 * pre * (1.0 + lax.erf(pre * (2.0 ** -0.5)))).astype(BF16)
            part = _dot(vt_ref[:, rs], act * gate_scr[rs, cols])
            out = part if out is None else out + part
        o_ref[:, cols] += out


def _peer_dense(h2, u, vt, r2, e2, cut, e1):
    t, d = h2.shape
    n_exp = u.shape[0]
    tm = _tile(t, 1024)
    te = 1024
    groups = PEER_N_KEYS // BF16_ROWS
    once = pl.Buffered(1)
    return pl.pallas_call(
        functools.partial(_peer_dense_kernel, rows=te // PEER_N_KEYS),
        out_shape=jax.ShapeDtypeStruct((d, t), F32),
        grid=(t // tm, n_exp // te),
        in_specs=[pl.BlockSpec((tm, d), lambda i, j: (i, 0), pipeline_mode=once),
                  pl.BlockSpec((te, d), lambda i, j: (j, 0)),
                  pl.BlockSpec((d, te), lambda i, j: (0, j)),
                  pl.BlockSpec((PEER_HEADS, groups, BF16_ROWS, tm), lambda i, j: (0, 0, 0, i), pipeline_mode=once),
                  pl.BlockSpec((PEER_HEADS, groups, BF16_ROWS, tm), lambda i, j: (0, 0, 0, i), pipeline_mode=once),
                  pl.BlockSpec((PEER_HEADS, PEER_N_KEYS, tm), lambda i, j: (0, 0, i), pipeline_mode=once),
                  pl.BlockSpec((PEER_HEADS, PEER_N_KEYS, tm), lambda i, j: (0, 0, i), pipeline_mode=once)],
        out_specs=pl.BlockSpec((d, tm), lambda i, j: (0, i)),
        scratch_shapes=[pltpu.VMEM((te, tm), F32), pltpu.VMEM((te, tm), BF16)],
        compiler_params=_params(("parallel", "arbitrary"), 60),
        name="peer_dense",
    )(h2, u, vt, r2, e2, cut, e1)


def _residual_kernel(x1_ref, pt_ref, mod_ref, gf_ref, o_ref, *, final_norm):
    x2 = x1_ref[...] + mod_ref[0, 5:6, :] * pt_ref[...].T
    o_ref[...] = _rms(x2) * gf_ref[...] if final_norm else x2


def _residual(x1, peer_t, mod3, gf, seq, final_norm):
    t, d = x1.shape
    tm = _tile(seq, 512)
    return pl.pallas_call(
        functools.partial(_residual_kernel, final_norm=final_norm),
        out_shape=jax.ShapeDtypeStruct((t, d), F32),
        grid=(t // tm,),
        in_specs=[pl.BlockSpec((tm, d), lambda i: (i, 0)),
                  pl.BlockSpec((d, tm), lambda i: (0, i)),
                  pl.BlockSpec((1, 6, d), lambda i: ((i * tm) // seq, 0, 0)),
                  pl.BlockSpec((1, d), lambda i: (0, 0))],
        out_specs=pl.BlockSpec((tm, d), lambda i: (i, 0)),
        compiler_params=_params(("parallel",), 40),
        name="residual",
    )(x1, peer_t, mod3, gf)


def _rot_half_cols(w):
    half = w.shape[-1] // 2
    return jnp.concatenate([-w[..., half:], w[..., :half]], axis=-1)


def _pad_cols(w, width):
    return jnp.concatenate([w, jnp.zeros(w.shape[:-1] + (width - w.shape[-1],), w.dtype)], axis=-1)


def _layer_weights(w_in, w_uq, w_ukv):
    n_heads_cols = 3 * SB_WIDTH
    w_kr = w_in[:, n_heads_cols + MLA_Q_RANK + MLA_KV_RANK:]
    w_latent = jnp.concatenate([w_in[:, n_heads_cols:n_heads_cols + MLA_Q_RANK + MLA_KV_RANK],
                                _pad_cols(w_kr, LANES), _pad_cols(_rot_half_cols(w_kr), LANES)], axis=1)
    wq = w_uq.reshape(MLA_Q_RANK, MLA_HEADS, MLA_QK_DIM)
    wq_pe = wq[..., MLA_NOPE_DIM:]
    wq_ext = jnp.concatenate([wq[..., :MLA_NOPE_DIM], _pad_cols(wq_pe, LANES),
                              _pad_cols(_rot_half_cols(wq_pe), LANES)], axis=-1)
    wq_ext = wq_ext.reshape(MLA_Q_RANK, MLA_HEADS * 3 * LANES)
    wkv = w_ukv.reshape(MLA_KV_RANK, MLA_HEADS, MLA_NOPE_DIM + MLA_V_DIM)
    wkn = wkv[..., :MLA_NOPE_DIM].reshape(MLA_KV_RANK, MLA_HEADS * MLA_NOPE_DIM)
    wv = wkv[..., MLA_NOPE_DIM:].reshape(MLA_KV_RANK, MLA_WIDTH)
    return w_in.astype(BF16), w_latent.astype(BF16), wq_ext.astype(BF16), wkn.astype(BF16), wv.astype(BF16)


def kernel(x, c, positions, w_ada, b_ada, norm1_g, w_in, mla_q_norm_g, w_uq, mla_kv_norm_g, w_ukv,
           out_norm_sba_g, out_norm_mla_g, w_o, norm2_g, w_peer_q, peer_keys1, peer_keys2,
           peer_u, peer_v, final_norm_g):
    batch, seq, d = x.shape
    depth = w_ada.shape[0]
    t = batch * seq
    x2d = x.reshape(t, d)
    pos = positions.reshape(t, 1).astype(F32)
    inv_freq = ROPE_THETA ** (-jnp.arange(0, MLA_ROPE_DIM, 2, dtype=F32) / MLA_ROPE_DIM)
    invf = _pad_cols(jnp.concatenate([inv_freq, inv_freq])[None, :], LANES)
    c_rows = 16
    c_pad = jnp.concatenate([c, jnp.zeros((c_rows - batch, d), c.dtype)], axis=0)
    row = lambda g: g[None, :]

    for l in range(depth):
        mod = _adaln(c_pad, w_ada[l], b_ada[l][None, :])
        mod3 = mod[:batch].reshape(batch, 6, d)
        w_heads, w_latent, wq_ext, wkn, wv = _layer_weights(w_in[l], w_uq[l], w_ukv[l])

        qkv, latent = _inproj(x2d, mod3, row(norm1_g[l]), w_heads, w_latent, seq)
        q_mla, k_mla, v_mla = _mla_proj(latent, pos, invf, row(mla_q_norm_g[l]), row(mla_kv_norm_g[l]),
                                        wq_ext, wkn, wv)
        sb_out = _sb_attn(qkv, batch, seq)
        mla_out = _mla_attn(q_mla, k_mla, v_mla, batch, seq)
        x1, h2 = _mix(sb_out, mla_out, x2d, mod3, row(out_norm_sba_g[l]), row(out_norm_mla_g[l]),
                      row(norm2_g[l]), w_o[l].astype(BF16), seq)
        routing = _peer_route(h2, w_peer_q[l].astype(BF16), peer_keys1[l].astype(BF16),
                              peer_keys2[l].astype(BF16))
        peer = _peer_dense(h2, peer_u[l].astype(BF16), peer_v[l].T.astype(BF16), *routing)
        x2d = _residual(x1, peer, mod3, row(final_norm_g), seq, final_norm=(l == depth - 1))
    return x2d.reshape(batch, seq, d)
```
